```python
import jax, jax.numpy as jnp
from jax import lax
import numpy as np

D_MODEL = 1024
BATCH = 16
SEQ = 4096
DEPTH = 2
DEC_BATCH = 8
DEC_SEQ = 64
PAST_LEN = 4096

CHUNK = 64
MLP_CHUNK = 128
D_MIX = D_MODEL
D_GMLP = D_MIX // 2
D_POOL = D_MIX - D_GMLP
GMLP_HEADS = 8
GMLP_HEAD_DIM = D_GMLP // GMLP_HEADS
POOL_WINDOWS = (2, 4, 8, 16)
POOL_GROUPS = len(POOL_WINDOWS)
POOL_GROUP_DIM = D_POOL // POOL_GROUPS
POOL_HIST = max(POOL_WINDOWS) - 1
D_IN_PROJ = 2 * D_GMLP + D_POOL
D_FF = ((8 * D_MODEL + 3 * 256 - 1) // (3 * 256)) * 256
EPS = 1e-6

kernel_name = "hybrid_gmlp_pool_streaming_encoder_step"


def rmsnorm(x, g):
    x32 = x.astype(jnp.float32)
    y = x32 * lax.rsqrt(jnp.mean(x32 * x32, axis=-1, keepdims=True) + EPS)
    return (y * g.astype(jnp.float32)).astype(x.dtype)


def gmlp_spatial(v, w_s, b_s):
    B, T, C = v.shape
    n_blk = -(-T // MLP_CHUNK)
    pad = n_blk * MLP_CHUNK - T
    vp = jnp.pad(v, ((0, 0), (0, pad), (0, 0))).reshape(B, n_blk, MLP_CHUNK, GMLP_HEADS, GMLP_HEAD_DIM)
    cid = jnp.arange(MLP_CHUNK) // CHUNK
    mask = cid[:, None] >= cid[None, :]
    w = jnp.where(mask[None], w_s, 0).astype(v.dtype)
    out = jnp.einsum('hij,bnjhc->bnihc', w, vp) + b_s.T.astype(v.dtype)[None, None, :, :, None]
    return out.reshape(B, n_blk * MLP_CHUNK, C)[:, :T]


def pool_mix(xb, hist, start_pos, w_pool, pool_scale):
    B, T, C = xb.shape
    xc = jnp.concatenate([hist, xb], axis=1).astype(jnp.float32)
    cs = jnp.pad(jnp.cumsum(xc, axis=1), ((0, 0), (1, 0), (0, 0)))
    pos = start_pos + jnp.arange(T)
    means = []
    for g, w in enumerate(POOL_WINDOWS):
        lo, hi = g * POOL_GROUP_DIM, (g + 1) * POOL_GROUP_DIM
        end = cs[:, POOL_HIST + 1:, lo:hi]
        beg = cs[:, POOL_HIST + 1 - w:POOL_HIST + 1 - w + T, lo:hi]
        cnt = jnp.minimum(pos + 1, w).astype(jnp.float32)
        means.append((end - beg) / cnt[None, :, None])
    mean = jnp.concatenate(means, axis=-1)
    d = (mean - xb.astype(jnp.float32)).astype(xb.dtype).reshape(B, T, POOL_GROUPS, POOL_GROUP_DIM)
    y = jnp.einsum('btgc,gcd->btgd', d, w_pool).reshape(B, T, C) * pool_scale
    new_hist = xc[:, -POOL_HIST:].astype(xb.dtype)
    return y, new_hist


def layer(x, hist, start_pos, g_mix, w_in, g_v, w_s, b_s, w_pool, pool_scale, g_branch, w_out,
          g_ffn, w_gate, w_up, w_down):
    h = rmsnorm(x, g_mix)
    z = h @ w_in
    u = jax.nn.gelu(z[..., :D_GMLP], approximate=False)
    v = rmsnorm(jax.nn.gelu(z[..., D_GMLP:2 * D_GMLP], approximate=False), g_v)
    xb = z[..., 2 * D_GMLP:]
    ya = u * gmlp_spatial(v, w_s, b_s)
    yb, new_hist = pool_mix(xb, hist, start_pos, w_pool, pool_scale)
    ym = jnp.concatenate([rmsnorm(ya, g_branch[:D_GMLP]), rmsnorm(yb, g_branch[D_GMLP:])], axis=-1)
    x = x + ym @ w_out
    h = rmsnorm(x, g_ffn)
    x = x + (jax.nn.silu(h @ w_gate) * (h @ w_up)) @ w_down
    return x, v, new_hist


def setup_inputs(seed: int = 0) -> dict:
    key = jax.random.key(seed)
    ks = jax.random.split(key, 17)
    f32 = jnp.float32

    def nrm(k, shape, scale):
        return jax.random.normal(k, shape, f32) * scale

    def gain(k, shape):
        return 1.0 + 0.05 * jax.random.normal(k, shape, f32)

    return {
        'x_prompt': nrm(ks[0], (BATCH, SEQ, D_MODEL), 1.0),
        'x_sample': nrm(ks[1], (DEC_BATCH, DEC_SEQ, D_MODEL), 1.0),
        'state_pool': nrm(ks[2], (DEPTH, DEC_BATCH, POOL_HIST, D_POOL), 1.0),
        'g_mix': gain(ks[3], (DEPTH, D_MODEL)),
        'w_in': nrm(ks[4], (DEPTH, D_MODEL, D_IN_PROJ), D_MODEL ** -0.5),
        'g_v': gain(ks[5], (DEPTH, D_GMLP)),
        'w_spatial': nrm(ks[6], (DEPTH, GMLP_HEADS, MLP_CHUNK, MLP_CHUNK), MLP_CHUNK ** -0.5),
        'b_spatial': gain(ks[7], (DEPTH, GMLP_HEADS, MLP_CHUNK)),
        'w_pool': nrm(ks[8], (DEPTH, POOL_GROUPS, POOL_GROUP_DIM, POOL_GROUP_DIM), POOL_GROUP_DIM ** -0.5),
        'pool_scale': gain(ks[9], (DEPTH, D_POOL)),
        'g_branch': gain(ks[10], (DEPTH, D_MIX)),
        'w_out': nrm(ks[11], (DEPTH, D_MIX, D_MODEL), D_MIX ** -0.5),
        'g_ffn': gain(ks[12], (DEPTH, D_MODEL)),
        'w_gate': nrm(ks[13], (DEPTH, D_MODEL, D_FF), D_MODEL ** -0.5),
        'w_up': nrm(ks[14], (DEPTH, D_MODEL, D_FF), D_MODEL ** -0.5),
        'w_down': nrm(ks[15], (DEPTH, D_FF, D_MODEL), D_FF ** -0.5),
        'g_final': gain(ks[16], (D_MODEL,)),
    }


def reference(x_prompt, x_sample, state_pool, g_mix, w_in, g_v, w_spatial, b_spatial, w_pool, pool_scale,
              g_branch, w_out, g_ffn, w_gate, w_up, w_down, g_final):
    xp, xs = x_prompt, x_sample
    hist_p_out, hist_s_out, v_s_out = [], [], []
    for l in range(DEPTH):
        params = (g_mix[l], w_in[l], g_v[l], w_spatial[l], b_spatial[l], w_pool[l], pool_scale[l],
                  g_branch[l], w_out[l], g_ffn[l], w_gate[l], w_up[l], w_down[l])
        hist0 = jnp.zeros((xp.shape[0], POOL_HIST, D_POOL), xp.dtype)
        xp, _, hp = layer(xp, hist0, 0, *params)
        xs, vs, hs = layer(xs, state_pool[l], PAST_LEN, *params)
        hist_p_out.append(hp)
        hist_s_out.append(hs)
        v_s_out.append(vs)
    y_prompt = rmsnorm(xp, g_final)
    y_sample = rmsnorm(xs, g_final)
    new_state_pool_prompt = jnp.stack(hist_p_out, axis=0)
    new_state_pool_sample = jnp.stack(hist_s_out, axis=0)
    new_gmlp_v_sample = jnp.stack(v_s_out, axis=0)
    return (y_prompt, y_sample, new_state_pool_prompt, new_state_pool_sample, new_gmlp_v_sample)
```

```python
import functools

import jax
import jax.numpy as jnp
from jax import lax
from jax.experimental import pallas as pl
from jax.experimental.pallas import tpu as pltpu

D_MODEL = 1024
D_GMLP = 512
D_POOL = 512
GMLP_HEADS = 8
MLP_CHUNK = 128
CHUNK = 64
POOL_WINDOWS = (2, 4, 8, 16)
POOL_GROUP_DIM = 128
POOL_HIST = 15
HIST_ROWS = 16
D_IN_PROJ = 2 * D_GMLP + D_POOL
D_FF = 2816
EPS = 1e-6
PAST_LEN = 4096

LANES = 128
FF_CHUNK = 256
ROW_CHUNK = 32
PROMPT_TILE = 512
VMEM_LIMIT_BYTES = 56 * 1024 * 1024


def _rms(x, g):
    ms = jnp.mean(x * x, axis=-1, keepdims=True)
    return x * lax.rsqrt(ms + EPS) * g


def _gelu(x):
    return 0.5 * x * (1.0 + lax.erf(x * (2.0 ** -0.5)))


def _silu(x):
    return x / (1.0 + jnp.exp(-x))


def _layer_kernel(x_ref, hist_ref, g_mix_ref, w_in_ref, g_v_ref, w_sp_ref, b_sp_ref, w_pool_ref,
                  pscale_ref, g_br_ref, w_out_ref, g_ffn_ref, w_gate_ref, w_up_ref, w_down_ref,
                  g_fin_ref, *rest, S, R, tiles_per_seq, pos0, final_norm, emit_v):
    if emit_v:
        y_ref, hist_out_ref, v_out_ref = rest[:3]
        scratch = rest[3:]
    else:
        y_ref, hist_out_ref = rest[:2]
        v_out_ref = None
        scratch = rest[2:]
    h_scr, z_scr, v_scr, d_scr, ym_scr, hid_scr, xc_scr = scratch

    M = S * R
    BR = min(R, MLP_CHUNK)
    f32, bf16 = jnp.float32, jnp.bfloat16
    tile = pl.program_id(1)

    g_mix = g_mix_ref[...]
    for r0 in range(0, M, ROW_CHUNK):
        rows = slice(r0, r0 + ROW_CHUNK)
        h_scr[rows, :] = _rms(x_ref[rows, :], g_mix).astype(bf16)

    z_scr[...] = jnp.dot(h_scr[...], w_in_ref[...], preferred_element_type=f32)

    g_v = g_v_ref[...]
    for r0 in range(0, M, ROW_CHUNK):
        rows = slice(r0, r0 + ROW_CHUNK)
        v = _rms(_gelu(z_scr[rows, D_GMLP:2 * D_GMLP]), g_v)
        if emit_v:
            v_out_ref[rows, :] = v
        v_scr[rows, :] = v

    ri = lax.broadcasted_iota(jnp.int32, (MLP_CHUNK, MLP_CHUNK), 0)
    ci = lax.broadcasted_iota(jnp.int32, (MLP_CHUNK, MLP_CHUNK), 1)
    causal = (ri // CHUNK) >= (ci // CHUNK)
    w_pairs = []
    for q in range(GMLP_HEADS // 2):
        wa = jnp.where(causal, w_sp_ref[2 * q], 0.0)[:BR].astype(bf16)
        wb = jnp.where(causal, w_sp_ref[2 * q + 1], 0.0)[:BR].astype(bf16)
        w_pairs.append(jnp.concatenate([wa, wb], axis=1))

    lane = lax.broadcasted_iota(jnp.int32, (BR, LANES), 1)
    first_head = lane < (LANES // 2)
    g_br_a = g_br_ref[:, 0:D_GMLP]
    for s in range(S):
        for b0 in range(0, R, BR):
            base = s * R + b0
            rows = slice(base, base + BR)
            sp = []
            for q in range(GMLP_HEADS // 2):
                vq = v_scr[rows, q * LANES:(q + 1) * LANES]
                lo = jnp.where(first_head, vq, 0.0).astype(bf16)
                hi = jnp.where(first_head, 0.0, vq).astype(bf16)
                if BR < MLP_CHUNK:
                    pad = jnp.zeros((MLP_CHUNK - BR, LANES), bf16)
                    rhs = jnp.concatenate([lo, pad, hi, pad], axis=0)
                else:
                    rhs = jnp.concatenate([lo, hi], axis=0)
                sp.append(jnp.dot(w_pairs[q], rhs, preferred_element_type=f32))
            sp = jnp.concatenate(sp, axis=1) + b_sp_ref[0:BR, :]
            ya = _gelu(z_scr[rows, 0:D_GMLP]) * sp
            ym_scr[rows, 0:D_GMLP] = _rms(ya, g_br_a).astype(bf16)

    for s in range(S):
        seq = slice(s * R, (s + 1) * R)
        if tiles_per_seq == 1:
            head = hist_ref[s]
        else:
            head = jnp.where(tile == 0, hist_ref[s], xc_scr[s, R:R + HIST_ROWS, :])
        xc_scr[s, 0:HIST_ROWS, :] = head
        xc_scr[s, HIST_ROWS:HIST_ROWS + R, :] = z_scr[seq, 2 * D_GMLP:]
        hist_out_ref[s] = xc_scr[s, R:R + HIST_ROWS, :]
        for g, w in enumerate(POOL_WINDOWS):
            cols = slice(g * POOL_GROUP_DIM, (g + 1) * POOL_GROUP_DIM)
            for r0 in range(0, R, ROW_CHUNK * 2):
                nr = ROW_CHUNK * 2
                xb = xc_scr[s, HIST_ROWS + r0:HIST_ROWS + r0 + nr, cols]
                acc = xb
                for k in range(1, w):
                    acc = acc + xc_scr[s, HIST_ROWS + r0 - k:HIST_ROWS + r0 - k + nr, cols]
                pos = pos0 + tile * R + r0 + lax.broadcasted_iota(jnp.int32, (nr, POOL_GROUP_DIM), 0)
                cnt = jnp.minimum(pos + 1, w).astype(f32)
                d_scr[s * R + r0:s * R + r0 + nr, cols] = (acc / cnt - xb).astype(bf16)
    for g in range(len(POOL_WINDOWS)):
        cols = slice(g * POOL_GROUP_DIM, (g + 1) * POOL_GROUP_DIM)
        yb = jnp.dot(d_scr[:, cols], w_pool_ref[g], preferred_element_type=f32)
        z_scr[:, 2 * D_GMLP + g * POOL_GROUP_DIM:2 * D_GMLP + (g + 1) * POOL_GROUP_DIM] = (
            yb * pscale_ref[:, cols])
    g_br_b = g_br_ref[:, D_GMLP:]
    for r0 in range(0, M, ROW_CHUNK):
        rows = slice(r0, r0 + ROW_CHUNK)
        ym_scr[rows, D_GMLP:] = _rms(z_scr[rows, 2 * D_GMLP:], g_br_b).astype(bf16)

    y_ref[...] = x_ref[...] + jnp.dot(ym_scr[...], w_out_ref[...], preferred_element_type=f32)

    g_ffn = g_ffn_ref[...]
    for r0 in range(0, M, ROW_CHUNK):
        rows = slice(r0, r0 + ROW_CHUNK)
        h_scr[rows, :] = _rms(y_ref[rows, :], g_ffn).astype(bf16)
    for c0 in range(0, D_FF, FF_CHUNK):
        cols = slice(c0, c0 + FF_CHUNK)
        gate = jnp.dot(h_scr[...], w_gate_ref[:, cols], preferred_element_type=f32)
        up = jnp.dot(h_scr[...], w_up_ref[:, cols], preferred_element_type=f32)
        hid_scr[:, cols] = (_silu(gate) * up).astype(bf16)
    y_ref[...] = y_ref[...] + jnp.dot(hid_scr[...], w_down_ref[...], preferred_element_type=f32)

    if final_norm:
        g_fin = g_fin_ref[...]
        for r0 in range(0, M, ROW_CHUNK):
            rows = slice(r0, r0 + ROW_CHUNK)
            y_ref[rows, :] = _rms(y_ref[rows, :], g_fin)


def _resident(shape):
    zeros = (0,) * len(shape)
    return pl.BlockSpec(shape, lambda b, i: zeros, pipeline_mode=pl.Buffered(1))


def _run_layer(x, hist, params, g_final, *, S, R, tiles_per_seq, pos0, final_norm, emit_v, name):
    n_rows = x.shape[0]
    M = S * R
    n_seq = n_rows // (R * tiles_per_seq)
    assert n_seq % S == 0 and (S == 1 or tiles_per_seq == 1)
    grid = (n_seq // S, tiles_per_seq)
    x_map = lambda b, i: (b * tiles_per_seq + i, 0)
    if hist.shape[0] == 1:
        hist_in_spec = pl.BlockSpec((S, HIST_ROWS, D_POOL), lambda b, i: (0, 0, 0))
    else:
        hist_in_spec = pl.BlockSpec((S, HIST_ROWS, D_POOL), lambda b, i: (b, 0, 0))
    hist_out_spec = pl.BlockSpec((S, HIST_ROWS, D_POOL), lambda b, i: (b, 0, 0))
    n_hist = n_seq

    (g_mix, w_in, g_v, w_sp, b_full, w_pool, pscale, g_br, w_out, g_ffn, w_gate, w_up, w_down) = params

    in_specs = [
        pl.BlockSpec((M, D_MODEL), x_map),
        hist_in_spec,
        _resident(g_mix.shape), _resident(w_in.shape), _resident(g_v.shape), _resident(w_sp.shape),
        _resident(b_full.shape), _resident(w_pool.shape), _resident(pscale.shape), _resident(g_br.shape),
        _resident(w_out.shape), _resident(g_ffn.shape), _resident(w_gate.shape), _resident(w_up.shape),
        _resident(w_down.shape), _resident(g_final.shape),
    ]
    out_shape = [jax.ShapeDtypeStruct((n_rows, D_MODEL), jnp.float32),
                 jax.ShapeDtypeStruct((n_hist, HIST_ROWS, D_POOL), jnp.float32)]
    out_specs = [pl.BlockSpec((M, D_MODEL), x_map), hist_out_spec]
    if emit_v:
        out_shape.append(jax.ShapeDtypeStruct((n_rows, D_GMLP), jnp.float32))
        out_specs.append(pl.BlockSpec((M, D_GMLP), x_map))

    scratch_shapes = [
        pltpu.VMEM((M, D_MODEL), jnp.bfloat16),
        pltpu.VMEM((M, D_IN_PROJ), jnp.float32),
        pltpu.VMEM((M, D_GMLP), jnp.float32),
        pltpu.VMEM((M, D_POOL), jnp.bfloat16),
        pltpu.VMEM((M, D_MODEL), jnp.bfloat16),
        pltpu.VMEM((M, D_FF), jnp.bfloat16),
        pltpu.VMEM((S, HIST_ROWS + R, D_POOL), jnp.float32),
    ]
    kern = functools.partial(_layer_kernel, S=S, R=R, tiles_per_seq=tiles_per_seq, pos0=pos0,
                             final_norm=final_norm, emit_v=emit_v)
    return pl.pallas_call(
        kern,
        grid=grid,
        in_specs=in_specs,
        out_specs=out_specs,
        out_shape=out_shape,
        scratch_shapes=scratch_shapes,
        compiler_params=pltpu.CompilerParams(
            dimension_semantics=("arbitrary", "arbitrary"),
            vmem_limit_bytes=VMEM_LIMIT_BYTES),
        name=name,
    )(x, hist, g_mix, w_in, g_v, w_sp, b_full, w_pool, pscale, g_br, w_out, g_ffn, w_gate, w_up,
      w_down, g_final)


def kernel(x_prompt, x_sample, state_pool, g_mix, w_in, g_v, w_spatial, b_spatial, w_pool, pool_scale,
           g_branch, w_out, g_ffn, w_gate, w_up, w_down, g_final):
    depth = w_in.shape[0]
    batch, seq, _ = x_prompt.shape
    dec_batch, dec_seq, _ = x_sample.shape
    bf16 = jnp.bfloat16

    xp = x_prompt.reshape(batch * seq, D_MODEL)
    xs = x_sample.reshape(dec_batch * dec_seq, D_MODEL)
    no_hist = jnp.zeros((1, HIST_ROWS, D_POOL), jnp.float32)
    g_fin = g_final.reshape(1, D_MODEL)

    hist_p, hist_s, v_s = [], [], []
    for l in range(depth):
        b_full = jnp.repeat(b_spatial[l].T, D_GMLP // GMLP_HEADS, axis=1)
        params = (g_mix[l].reshape(1, -1), w_in[l].astype(bf16), g_v[l].reshape(1, -1), w_spatial[l],
                  b_full, w_pool[l].astype(bf16), pool_scale[l].reshape(1, -1),
                  g_branch[l].reshape(1, -1), w_out[l].astype(bf16), g_ffn[l].reshape(1, -1),
                  w_gate[l].astype(bf16), w_up[l].astype(bf16), w_down[l].astype(bf16))
        last = l == depth - 1
        xp, hp = _run_layer(xp, no_hist, params, g_fin, S=1, R=PROMPT_TILE,
                            tiles_per_seq=seq // PROMPT_TILE, pos0=0,
                            final_norm=last, emit_v=False, name=f"prompt_layer{l}")
        hs_in = jnp.pad(state_pool[l], ((0, 0), (HIST_ROWS - POOL_HIST, 0), (0, 0)))
        xs, hs, vs = _run_layer(xs, hs_in, params, g_fin, S=dec_batch, R=dec_seq, tiles_per_seq=1,
                                pos0=PAST_LEN, final_norm=last, emit_v=True, name=f"sample_layer{l}")
        hist_p.append(hp[:, HIST_ROWS - POOL_HIST:, :])
        hist_s.append(hs[:, HIST_ROWS - POOL_HIST:, :])
        v_s.append(vs.reshape(dec_batch, dec_seq, D_GMLP))

    y_prompt = xp.reshape(batch, seq, D_MODEL)
    y_sample = xs.reshape(dec_batch, dec_seq, D_MODEL)
    return (y_prompt, y_sample, jnp.stack(hist_p, axis=0), jnp.stack(hist_s, axis=0),
            jnp.stack(v_s, axis=0))
```

```python
import functools

import jax
import jax.numpy as jnp
from jax import lax
from jax.experimental import pallas as pl
from jax.experimental.pallas import tpu as pltpu

D_MODEL = 1024
D_GMLP = 512
D_POOL = 512
GMLP_HEADS = 8
MLP_CHUNK = 128
CHUNK = 64
POOL_WINDOWS = (2, 4, 8, 16)
POOL_GROUP_DIM = 128
POOL_HIST = 15
HIST_ROWS = 16
D_IN_PROJ = 2 * D_GMLP + D_POOL
D_FF = 2816
EPS = 1e-6
PAST_LEN = 4096

LANES = 128
FF_CHUNK = 256
PROJ_CHUNK = 512
ROW_CHUNK = 32
UNIT_ROWS = 128
POOL_ROW_CHUNK = 64
PROMPT_TILE = 512
VMEM_LIMIT_BYTES = 56 * 1024 * 1024


def _rms(x, g):
    ms = jnp.mean(x * x, axis=-1, keepdims=True)
    return x * lax.rsqrt(ms + EPS) * g


def _gelu(x):
    return 0.5 * x * (1.0 + lax.erf(x * (2.0 ** -0.5)))


def _silu(x):
    return x / (1.0 + jnp.exp(-x))


def _front_units(x_ref, hist_ref, p, scr, x1_ref, h2_ref, hist_out_ref, v_out_ref, *,
                 S, R, first_tile, pos_base):
    h_scr, z_scr, v_scr, d_scr, ym_scr, xc_scr, wsp_scr = scr
    M = S * R
    BR = min(R, MLP_CHUNK)
    f32, bf16 = jnp.float32, jnp.bfloat16
    pre = []

    def rows_units(fn):
        def unit(u0):
            def run():
                for r0 in range(u0, u0 + UNIT_ROWS, ROW_CHUNK):
                    fn(slice(r0, r0 + ROW_CHUNK))
            return run
        return [unit(u0) for u0 in range(0, M, UNIT_ROWS)]

    def norm1(rows):
        h_scr[rows, :] = _rms(x_ref[rows, :], p["g_mix"][...]).astype(bf16)
    pre += rows_units(norm1)

    def in_proj(c0):
        def run():
            cols = slice(c0, c0 + PROJ_CHUNK)
            z_scr[:, cols] = jnp.dot(h_scr[...], p["w_in"][:, cols], preferred_element_type=f32)
        return run
    pre += [in_proj(D_GMLP), in_proj(0), in_proj(2 * D_GMLP)]

    def v_norm(rows):
        v = _rms(_gelu(z_scr[rows, D_GMLP:2 * D_GMLP]), p["g_v"][...])
        if v_out_ref is not None:
            v_out_ref[rows, :] = v
        v_scr[rows, :] = v
    pre += rows_units(v_norm)

    def pool_rows():
        for s in range(S):
            if first_tile is None:
                head = hist_ref[s]
            else:
                head = jnp.where(first_tile, hist_ref[s], xc_scr[s, R:R + HIST_ROWS, :])
            xc_scr[s, 0:HIST_ROWS, :] = head
            xc_scr[s, HIST_ROWS:HIST_ROWS + R, :] = z_scr[s * R:(s + 1) * R, 2 * D_GMLP:]
            hist_out_ref[s] = xc_scr[s, R:R + HIST_ROWS, :]
    pre.append(pool_rows)

    def pool_windows(s, q0, nr):
        def run():
            for g, w in enumerate(POOL_WINDOWS):
                cols = slice(g * POOL_GROUP_DIM, (g + 1) * POOL_GROUP_DIM)
                for r0 in range(q0, q0 + nr, POOL_ROW_CHUNK):
                    n = POOL_ROW_CHUNK
                    xb = xc_scr[s, HIST_ROWS + r0:HIST_ROWS + r0 + n, cols]
                    acc = xb
                    for k in range(1, w):
                        acc = acc + xc_scr[s, HIST_ROWS + r0 - k:HIST_ROWS + r0 - k + n, cols]
                    pos = pos_base + r0 + lax.broadcasted_iota(jnp.int32, (n, POOL_GROUP_DIM), 0)
                    cnt = jnp.minimum(pos + 1, w).astype(f32)
                    d_scr[s * R + r0:s * R + r0 + n, cols] = (acc / cnt - xb).astype(bf16)
        return run
    pool_w = [pool_windows(s, q0, min(R, UNIT_ROWS)) for s in range(S) for q0 in range(0, R, UNIT_ROWS)]

    def spatial(b0):
        def run():
            lane = lax.broadcasted_iota(jnp.int32, (BR, LANES), 1)
            first_head = lane < (LANES // 2)
            rows = slice(b0, b0 + BR)
            sp = []
            for q in range(GMLP_HEADS // 2):
                vq = v_scr[rows, q * LANES:(q + 1) * LANES]
                lo = jnp.where(first_head, vq, 0.0).astype(bf16)
                hi = jnp.where(first_head, 0.0, vq).astype(bf16)
                if BR < MLP_CHUNK:
                    pad = jnp.zeros((MLP_CHUNK - BR, LANES), bf16)
                    rhs = jnp.concatenate([lo, pad, hi, pad], axis=0)
                else:
                    rhs = jnp.concatenate([lo, hi], axis=0)
                sp.append(jnp.dot(wsp_scr[q, 0:BR, :], rhs, preferred_element_type=f32))
            sp = jnp.concatenate(sp, axis=1) + p["b_sp"][0:BR, :]
            ya = _gelu(z_scr[rows, 0:D_GMLP]) * sp
            ym_scr[rows, 0:D_GMLP] = _rms(ya, p["g_br"][:, 0:D_GMLP]).astype(bf16)
        return run
    spatial_u = [spatial(b0) for b0 in range(0, M, BR)]
    n_mix = max(len(spatial_u), len(pool_w))
    for j in range(n_mix):
        pre += spatial_u[j * len(spatial_u) // n_mix:(j + 1) * len(spatial_u) // n_mix]
        pre += pool_w[j * len(pool_w) // n_mix:(j + 1) * len(pool_w) // n_mix]

    def pool_proj():
        for g in range(len(POOL_WINDOWS)):
            cols = slice(g * POOL_GROUP_DIM, (g + 1) * POOL_GROUP_DIM)
            yb = jnp.dot(d_scr[:, cols], p["w_pool"][g], preferred_element_type=f32)
            z_scr[:, 2 * D_GMLP + g * POOL_GROUP_DIM:2 * D_GMLP + (g + 1) * POOL_GROUP_DIM] = (
                yb * p["pscale"][:, cols])
    pre.append(pool_proj)

    def pool_norm(rows):
        ym_scr[rows, D_GMLP:] = _rms(z_scr[rows, 2 * D_GMLP:], p["g_br"][:, D_GMLP:]).astype(bf16)
    pre += rows_units(pool_norm)

    def out_proj(c0):
        def run():
            cols = slice(c0, c0 + PROJ_CHUNK)
            x1_ref[:, cols] = x_ref[:, cols] + jnp.dot(ym_scr[...], p["w_out"][:, cols],
                                                       preferred_element_type=f32)
        return run
    pre += [out_proj(c0) for c0 in range(0, D_MODEL, PROJ_CHUNK)]

    def norm2(rows):
        h2_ref[rows, :] = _rms(x1_ref[rows, :], p["g_ffn"][...]).astype(bf16)
    return pre, rows_units(norm2)


def _ffn_units(x1_ref, h2_ref, y_ref, p, hid_scr, *, M, final_norm):
    f32, bf16 = jnp.float32, jnp.bfloat16

    def gate_up(c0):
        def run():
            cols = slice(c0, c0 + FF_CHUNK)
            gate = jnp.dot(h2_ref[...], p["w_gate"][:, cols], preferred_element_type=f32)
            up = jnp.dot(h2_ref[...], p["w_up"][:, cols], preferred_element_type=f32)
            hid_scr[:, cols] = (_silu(gate) * up).astype(bf16)
        return run

    def down(c0):
        def run():
            cols = slice(c0, c0 + PROJ_CHUNK)
            y_ref[:, cols] = x1_ref[:, cols] + jnp.dot(hid_scr[...], p["w_down"][:, cols],
                                                       preferred_element_type=f32)
        return run

    def fin(u0):
        def run():
            for r0 in range(u0, u0 + UNIT_ROWS, ROW_CHUNK):
                rows = slice(r0, r0 + ROW_CHUNK)
                y_ref[rows, :] = _rms(y_ref[rows, :], p["g_fin"][...])
        return run

    chunks = [gate_up(c0) for c0 in range(0, D_FF, FF_CHUNK)]
    downs = [down(c0) for c0 in range(0, D_MODEL, PROJ_CHUNK)]
    fins = [fin(u0) for u0 in range(0, M, UNIT_ROWS)] if final_norm else []
    return chunks, downs, fins


def _stack_spatial_weights(w_sp_ref, wsp_scr):
    ri = lax.broadcasted_iota(jnp.int32, (MLP_CHUNK, MLP_CHUNK), 0)
    ci = lax.broadcasted_iota(jnp.int32, (MLP_CHUNK, MLP_CHUNK), 1)
    causal = (ri // CHUNK) >= (ci // CHUNK)
    for h in range(GMLP_HEADS):
        half = (h % 2) * MLP_CHUNK
        wsp_scr[h // 2, :, half:half + MLP_CHUNK] = jnp.where(causal, w_sp_ref[h], 0.0).astype(jnp.bfloat16)


_PARAM_NAMES = ("g_mix", "w_in", "g_v", "w_sp", "b_sp", "w_pool", "pscale", "g_br", "w_out", "g_ffn",
                "w_gate", "w_up", "w_down", "g_fin")


def _layer_kernel(x_ref, hist_ref, *rest, S, R, tiles_per_seq, n_tiles, pos0, final_norm, emit_v,
                  pipelined):
    p = dict(zip(_PARAM_NAMES, rest))
    rest = rest[len(_PARAM_NAMES):]
    y_ref, hist_out_ref = rest[:2]
    v_out_ref = rest[2] if emit_v else None
    x1_scr, h2_scr, hid_scr = rest[3 if emit_v else 2:][:3]
    front_scr = rest[3 if emit_v else 2:][3:]
    M = S * R

    wsp_scr = front_scr[-1]

    if not pipelined:
        _stack_spatial_weights(p["w_sp"], wsp_scr)
        pre, norm2 = _front_units(x_ref, hist_ref, p, front_scr, x1_scr.at[0], h2_scr.at[0], hist_out_ref,
                                  v_out_ref, S=S, R=R, first_tile=None, pos_base=pos0)
        chunks, downs, fins = _ffn_units(x1_scr.at[0], h2_scr.at[0], y_ref, p, hid_scr, M=M,
                                         final_norm=final_norm)
        for unit in pre + norm2 + chunks + downs + fins:
            unit()
        return

    step = pl.program_id(0)
    cur = lax.rem(step, 2)
    prev = 1 - cur

    @pl.when(step == 0)
    def _():
        x1_scr[...] = jnp.zeros(x1_scr.shape, x1_scr.dtype)
        h2_scr[...] = jnp.zeros(h2_scr.shape, h2_scr.dtype)
        _stack_spatial_weights(p["w_sp"], wsp_scr)

    tile_in_seq = lax.rem(jnp.minimum(step, n_tiles - 1), tiles_per_seq)
    pre, norm2 = _front_units(x_ref, hist_ref, p, front_scr, x1_scr.at[cur], h2_scr.at[cur], hist_out_ref,
                              v_out_ref, S=S, R=R, first_tile=tile_in_seq == 0,
                              pos_base=pos0 + tile_in_seq * R)
    chunks, downs, fins = _ffn_units(x1_scr.at[prev], h2_scr.at[prev], y_ref, p, hid_scr, M=M,
                                     final_norm=final_norm)
    n = len(chunks)
    for j in range(n):
        for unit in pre[j * len(pre) // n:(j + 1) * len(pre) // n]:
            unit()
        chunks[j]()
    downs[0]()
    for unit in norm2:
        unit()
    for unit in downs[1:] + fins:
        unit()


def _resident(shape, n_grid):
    zeros = (0,) * len(shape)
    index_map = (lambda i: zeros) if n_grid == 1 else (lambda b, i: zeros)
    return pl.BlockSpec(shape, index_map, pipeline_mode=pl.Buffered(1))


def _run_layer(x, hist, params, *, S, R, tiles_per_seq, pos0, final_norm, emit_v, pipelined, name):
    n_rows = x.shape[0]
    M = S * R
    n_seq = n_rows // (R * tiles_per_seq)
    n_tiles = n_rows // M
    assert n_seq % S == 0 and (S == 1 or tiles_per_seq == 1)
    assert M % UNIT_ROWS == 0 and (R % UNIT_ROWS == 0 or R == POOL_ROW_CHUNK)
    shared_hist = hist.shape[0] == 1
    if pipelined:
        assert S == 1
        last = n_tiles - 1
        grid = (n_tiles + 1,)
        x_map = lambda i: (jnp.minimum(i, last), 0)
        y_map = lambda i: (jnp.maximum(i - 1, 0), 0)
        hist_in_map = lambda i: (0 if shared_hist else jnp.minimum(i, last) // tiles_per_seq, 0, 0)
        hist_out_map = lambda i: (jnp.minimum(i, last) // tiles_per_seq, 0, 0)
        semantics = ("arbitrary",)
    else:
        grid = (n_seq // S, tiles_per_seq)
        assert tiles_per_seq == 1, "the sequential form carries no pooling history between tiles"
        x_map = y_map = lambda b, i: (b, 0)
        hist_in_map = lambda b, i: (0 if shared_hist else b, 0, 0)
        hist_out_map = lambda b, i: (b, 0, 0)
        semantics = ("arbitrary", "arbitrary")

    in_specs = [pl.BlockSpec((M, D_MODEL), x_map),
                pl.BlockSpec((S, HIST_ROWS, D_POOL), hist_in_map)]
    in_specs += [_resident(params[k].shape, len(grid)) for k in _PARAM_NAMES]
    out_shape = [jax.ShapeDtypeStruct((n_rows, D_MODEL), jnp.float32),
                 jax.ShapeDtypeStruct((n_seq, HIST_ROWS, D_POOL), jnp.float32)]
    out_specs = [pl.BlockSpec((M, D_MODEL), y_map),
                 pl.BlockSpec((S, HIST_ROWS, D_POOL), hist_out_map)]
    if emit_v:
        assert not pipelined
        out_shape.append(jax.ShapeDtypeStruct((n_rows, D_GMLP), jnp.float32))
        out_specs.append(pl.BlockSpec((M, D_GMLP), x_map))

    n_slots = 2 if pipelined else 1
    scratch_shapes = [
        pltpu.VMEM((n_slots, M, D_MODEL), jnp.float32),
        pltpu.VMEM((n_slots, M, D_MODEL), jnp.bfloat16),
        pltpu.VMEM((M, D_FF), jnp.bfloat16),
        pltpu.VMEM((M, D_MODEL), jnp.bfloat16),
        pltpu.VMEM((M, D_IN_PROJ), jnp.float32),
        pltpu.VMEM((M, D_GMLP), jnp.float32),
        pltpu.VMEM((M, D_POOL), jnp.bfloat16),
        pltpu.VMEM((M, D_MODEL), jnp.bfloat16),
        pltpu.VMEM((S, HIST_ROWS + R, D_POOL), jnp.float32),
        pltpu.VMEM((GMLP_HEADS // 2, MLP_CHUNK, 2 * MLP_CHUNK), jnp.bfloat16),
    ]
    kern = functools.partial(_layer_kernel, S=S, R=R, tiles_per_seq=tiles_per_seq, n_tiles=n_tiles,
                             pos0=pos0, final_norm=final_norm, emit_v=emit_v, pipelined=pipelined)
    return pl.pallas_call(
        kern,
        grid=grid,
        in_specs=in_specs,
        out_specs=out_specs,
        out_shape=out_shape,
        scratch_shapes=scratch_shapes,
        compiler_params=pltpu.CompilerParams(
            dimension_semantics=semantics,
            vmem_limit_bytes=VMEM_LIMIT_BYTES),
        name=name,
    )(x, hist, *[params[k] for k in _PARAM_NAMES])


def kernel(x_prompt, x_sample, state_pool, g_mix, w_in, g_v, w_spatial, b_spatial, w_pool, pool_scale,
           g_branch, w_out, g_ffn, w_gate, w_up, w_down, g_final):
    depth = w_in.shape[0]
    batch, seq, _ = x_prompt.shape
    dec_batch, dec_seq, _ = x_sample.shape
    bf16 = jnp.bfloat16

    xp = x_prompt.reshape(batch * seq, D_MODEL)
    xs = x_sample.reshape(dec_batch * dec_seq, D_MODEL)
    no_hist = jnp.zeros((1, HIST_ROWS, D_POOL), jnp.float32)

    hist_p, hist_s, v_s = [], [], []
    for l in range(depth):
        params = {
            "g_mix": g_mix[l].reshape(1, -1), "w_in": w_in[l].astype(bf16), "g_v": g_v[l].reshape(1, -1),
            "w_sp": w_spatial[l],
            "b_sp": jnp.repeat(b_spatial[l].T, D_GMLP // GMLP_HEADS, axis=1),
            "w_pool": w_pool[l].astype(bf16), "pscale": pool_scale[l].reshape(1, -1),
            "g_br": g_branch[l].reshape(1, -1), "w_out": w_out[l].astype(bf16),
            "g_ffn": g_ffn[l].reshape(1, -1), "w_gate": w_gate[l].astype(bf16),
            "w_up": w_up[l].astype(bf16), "w_down": w_down[l].astype(bf16),
            "g_fin": g_final.reshape(1, -1),
        }
        last = l == depth - 1
        xp, hp = _run_layer(xp, no_hist, params, S=1, R=PROMPT_TILE, tiles_per_seq=seq // PROMPT_TILE,
                            pos0=0, final_norm=last, emit_v=False, pipelined=True,
                            name=f"prompt_layer{l}")
        hs_in = jnp.pad(state_pool[l], ((0, 0), (HIST_ROWS - POOL_HIST, 0), (0, 0)))
        xs, hs, vs = _run_layer(xs, hs_in, params, S=dec_batch, R=dec_seq, tiles_per_seq=1,
                                pos0=PAST_LEN, final_norm=last, emit_v=True, pipelined=False,
                                name=f"sample_layer{l}")
        hist_p.append(hp[:, HIST_ROWS - POOL_HIST:, :])
        hist_s.append(hs[:, HIST_ROWS - POOL_HIST:, :])
        v_s.append(vs.reshape(dec_batch, dec_seq, D_GMLP))

    y_prompt = xp.reshape(batch, seq, D_MODEL)
    y_sample = xs.reshape(dec_batch, dec_seq, D_MODEL)
    return (y_prompt, y_sample, jnp.stack(hist_p, axis=0), jnp.stack(hist_s, axis=0),
            jnp.stack(v_s, axis=0))
```

```python
import functools

import jax
import jax.numpy as jnp
from jax import lax
from jax.experimental import pallas as pl
from jax.experimental.pallas import tpu as pltpu

D_MODEL = 1024
D_GMLP = 512
D_POOL = 512
GMLP_HEADS = 8
MLP_CHUNK = 128
CHUNK = 64
POOL_WINDOWS = (2, 4, 8, 16)
POOL_GROUP_DIM = 128
POOL_HIST = 15
HIST_ROWS = 16
D_IN_PROJ = 2 * D_GMLP + D_POOL
D_FF = 2816
EPS = 1e-6
PAST_LEN = 4096

LANES = 128
FF_CHUNK = 256
PROJ_CHUNK = 512
ROW_CHUNK = 32
UNIT_ROWS = 128
POOL_ROW_CHUNK = 64
PROMPT_TILE = 512
VMEM_LIMIT_BYTES = 56 * 1024 * 1024


def _rms(x, g):
    ms = jnp.mean(x * x, axis=-1, keepdims=True)
    return x * lax.rsqrt(ms + EPS) * g


def _gelu(x):
    return 0.5 * x * (1.0 + lax.erf(x * (2.0 ** -0.5)))


def _silu(x):
    return x / (1.0 + jnp.exp(-x))


def _front_units(x_ref, hist_ref, p, scr, x1_ref, h2_ref, hist_out_ref, v_out_ref, *,
                 S, R, first_tile, pos_base):
    h_scr, z_scr, v_scr, d_scr, ym_scr, xc_scr, wsp_scr, wpool_scr = scr
    M = S * R
    BR = min(R, MLP_CHUNK)
    f32, bf16 = jnp.float32, jnp.bfloat16
    stages = {}

    def rows_units(fn):
        def unit(u0):
            def run():
                for r0 in range(u0, u0 + UNIT_ROWS, ROW_CHUNK):
                    fn(slice(r0, r0 + ROW_CHUNK))
            return run
        return [unit(u0) for u0 in range(0, M, UNIT_ROWS)]

    def norm1(rows):
        h_scr[rows, :] = _rms(x_ref[rows, :], p["g_mix"][...]).astype(bf16)
    stages["norm1"] = rows_units(norm1)

    def in_proj(c0):
        def run():
            cols = slice(c0, c0 + PROJ_CHUNK)
            z_scr[:, cols] = jnp.dot(h_scr[...], p["w_in"][:, cols], preferred_element_type=f32)
        return run
    stages["in_proj"] = [in_proj(D_GMLP), in_proj(0), in_proj(2 * D_GMLP)]

    def v_norm(rows):
        v = _rms(_gelu(z_scr[rows, D_GMLP:2 * D_GMLP]), p["g_v"][...])
        if v_out_ref is not None:
            v_out_ref[rows, :] = v
        v_scr[rows, :] = v
    stages["v_norm"] = rows_units(v_norm)

    def pool_rows():
        for s in range(S):
            if first_tile is None:
                head = hist_ref[s]
            else:
                head = jnp.where(first_tile, hist_ref[s], xc_scr[s, R:R + HIST_ROWS, :])
            xc_scr[s, 0:HIST_ROWS, :] = head
            xc_scr[s, HIST_ROWS:HIST_ROWS + R, :] = z_scr[s * R:(s + 1) * R, 2 * D_GMLP:]
            hist_out_ref[s] = xc_scr[s, R:R + HIST_ROWS, :]
    stages["pool_rows"] = [pool_rows]

    def pool_windows(s, q0, nr):
        def run():
            for g, w in enumerate(POOL_WINDOWS):
                cols = slice(g * POOL_GROUP_DIM, (g + 1) * POOL_GROUP_DIM)
                for r0 in range(q0, q0 + nr, POOL_ROW_CHUNK):
                    n = POOL_ROW_CHUNK
                    xb = xc_scr[s, HIST_ROWS + r0:HIST_ROWS + r0 + n, cols]
                    acc = xb
                    for k in range(1, w):
                        acc = acc + xc_scr[s, HIST_ROWS + r0 - k:HIST_ROWS + r0 - k + n, cols]
                    pos = pos_base + r0 + lax.broadcasted_iota(jnp.int32, (n, POOL_GROUP_DIM), 0)
                    cnt = jnp.minimum(pos + 1, w).astype(f32)
                    d_scr[s * R + r0:s * R + r0 + n, cols] = (acc / cnt - xb).astype(bf16)
        return run
    stages["pool_windows"] = [pool_windows(s, q0, min(R, UNIT_ROWS))
                              for s in range(S) for q0 in range(0, R, UNIT_ROWS)]

    def spatial(b0):
        def run():
            lane = lax.broadcasted_iota(jnp.int32, (BR, LANES), 1)
            first_head = lane < (LANES // 2)
            blocks = [slice(b0 + j * BR, b0 + (j + 1) * BR) for j in range(2)]
            sp = [[], []]
            for q in range(GMLP_HEADS // 2):
                rhs = []
                for rows in blocks:
                    vq = v_scr[rows, q * LANES:(q + 1) * LANES]
                    lo = jnp.where(first_head, vq, 0.0).astype(bf16)
                    hi = jnp.where(first_head, 0.0, vq).astype(bf16)
                    if BR < MLP_CHUNK:
                        pad = jnp.zeros((MLP_CHUNK - BR, LANES), bf16)
                        rhs.append(jnp.concatenate([lo, pad, hi, pad], axis=0))
                    else:
                        rhs.append(jnp.concatenate([lo, hi], axis=0))
                res = jnp.dot(wsp_scr[q, 0:BR, :], jnp.concatenate(rhs, axis=1),
                              preferred_element_type=f32)
                for j in range(2):
                    sp[j].append(res[:, j * LANES:(j + 1) * LANES])
            for j, rows in enumerate(blocks):
                spj = jnp.concatenate(sp[j], axis=1) + p["b_sp"][0:BR, :]
                ya = _gelu(z_scr[rows, 0:D_GMLP]) * spj
                ym_scr[rows, 0:D_GMLP] = _rms(ya, p["g_br"][:, 0:D_GMLP]).astype(bf16)
        return run
    stages["spatial"] = [spatial(b0) for b0 in range(0, M, 2 * BR)]

    def pool_proj():
        for j in range(len(POOL_WINDOWS) // 2):
            cols = slice(2 * j * POOL_GROUP_DIM, (2 * j + 2) * POOL_GROUP_DIM)
            yb = jnp.dot(d_scr[:, cols], wpool_scr[j], preferred_element_type=f32)
            z_scr[:, 2 * D_GMLP + cols.start:2 * D_GMLP + cols.stop] = yb * p["pscale"][:, cols]
    stages["pool_proj"] = [pool_proj]

    def pool_norm(rows):
        ym_scr[rows, D_GMLP:] = _rms(z_scr[rows, 2 * D_GMLP:], p["g_br"][:, D_GMLP:]).astype(bf16)
    stages["pool_norm"] = rows_units(pool_norm)

    def out_proj(c0):
        def run():
            cols = slice(c0, c0 + PROJ_CHUNK)
            x1_ref[:, cols] = x_ref[:, cols] + jnp.dot(ym_scr[...], p["w_out"][:, cols],
                                                       preferred_element_type=f32)
        return run
    stages["out_proj"] = [out_proj(c0) for c0 in range(0, D_MODEL, PROJ_CHUNK)]

    def norm2(rows):
        h2_ref[rows, :] = _rms(x1_ref[rows, :], p["g_ffn"][...]).astype(bf16)
    stages["norm2"] = rows_units(norm2)
    return stages


def _ffn_units(x1_ref, h2_ref, y_ref, p, hid_scr, *, M, final_norm):
    f32, bf16 = jnp.float32, jnp.bfloat16

    def gate_up(c0):
        def run():
            cols = slice(c0, c0 + FF_CHUNK)
            gate = jnp.dot(h2_ref[...], p["w_gate"][:, cols], preferred_element_type=f32)
            up = jnp.dot(h2_ref[...], p["w_up"][:, cols], preferred_element_type=f32)
            hid_scr[:, cols] = (_silu(gate) * up).astype(bf16)
        return run

    def down(c0):
        def run():
            cols = slice(c0, c0 + PROJ_CHUNK)
            y_ref[:, cols] = x1_ref[:, cols] + jnp.dot(hid_scr[...], p["w_down"][:, cols],
                                                       preferred_element_type=f32)
        return run

    def fin(u0):
        def run():
            for r0 in range(u0, u0 + UNIT_ROWS, ROW_CHUNK):
                rows = slice(r0, r0 + ROW_CHUNK)
                y_ref[rows, :] = _rms(y_ref[rows, :], p["g_fin"][...])
        return run

    chunks = [gate_up(c0) for c0 in range(0, D_FF, FF_CHUNK)]
    downs = [down(c0) for c0 in range(0, D_MODEL, PROJ_CHUNK)]
    fins = [fin(u0) for u0 in range(0, M, UNIT_ROWS)] if final_norm else []
    return chunks, downs, fins


_FRONT_ORDER = ("norm1", "in_proj", "v_norm", "pool_rows", "pool_windows", "spatial", "pool_proj",
                "pool_norm", "out_proj", "norm2")

_PIPELINE_ORDER = (
    "norm1", "norm1", "ffn", "norm1", "norm1", "ffn", "in_proj",
    "v_norm", "in_proj", "v_norm", "ffn", "v_norm", "in_proj", "v_norm", "ffn",
    "pool_rows", "spatial", "ffn", "pool_windows", "pool_windows", "ffn", "spatial", "ffn",
    "pool_windows", "pool_windows", "ffn", "pool_proj", "pool_norm", "pool_norm", "pool_norm", "pool_norm",
    "ffn", "ffn", "ffn", "out_proj", "out_proj", "down", "norm2", "norm2", "norm2", "norm2", "down")


def _stack_small_weights(w_sp_ref, w_pool_ref, wsp_scr, wpool_scr):
    ri = lax.broadcasted_iota(jnp.int32, (MLP_CHUNK, MLP_CHUNK), 0)
    ci = lax.broadcasted_iota(jnp.int32, (MLP_CHUNK, MLP_CHUNK), 1)
    causal = (ri // CHUNK) >= (ci // CHUNK)
    for h in range(GMLP_HEADS):
        half = (h % 2) * MLP_CHUNK
        wsp_scr[h // 2, :, half:half + MLP_CHUNK] = jnp.where(causal, w_sp_ref[h], 0.0).astype(jnp.bfloat16)
    G = POOL_GROUP_DIM
    zeros = jnp.zeros((G, G), jnp.bfloat16)
    for g in range(len(POOL_WINDOWS)):
        a = (g % 2) * G
        wpool_scr[g // 2, a:a + G, a:a + G] = w_pool_ref[g]
        wpool_scr[g // 2, a:a + G, G - a:2 * G - a] = zeros


_PARAM_NAMES = ("g_mix", "w_in", "g_v", "w_sp", "b_sp", "w_pool", "pscale", "g_br", "w_out", "g_ffn",
                "w_gate", "w_up", "w_down", "g_fin")


def _layer_kernel(x_ref, hist_ref, *rest, S, R, tiles_per_seq, n_tiles, pos0, final_norm, emit_v,
                  pipelined):
    p = dict(zip(_PARAM_NAMES, rest))
    rest = rest[len(_PARAM_NAMES):]
    y_ref, hist_out_ref = rest[:2]
    v_out_ref = rest[2] if emit_v else None
    x1_scr, h2_scr, hid_scr = rest[3 if emit_v else 2:][:3]
    front_scr = rest[3 if emit_v else 2:][3:]
    M = S * R

    wsp_scr, wpool_scr = front_scr[-2:]

    if not pipelined:
        _stack_small_weights(p["w_sp"], p["w_pool"], wsp_scr, wpool_scr)
        stages = _front_units(x_ref, hist_ref, p, front_scr, x1_scr.at[0], h2_scr.at[0], hist_out_ref,
                              v_out_ref, S=S, R=R, first_tile=None, pos_base=pos0)
        chunks, downs, fins = _ffn_units(x1_scr.at[0], h2_scr.at[0], y_ref, p, hid_scr, M=M,
                                         final_norm=final_norm)
        for unit in [u for name in _FRONT_ORDER for u in stages[name]] + chunks + downs + fins:
            unit()
        return

    step = pl.program_id(0)
    cur = lax.rem(step, 2)
    prev = 1 - cur

    @pl.when(step == 0)
    def _():
        x1_scr[...] = jnp.zeros(x1_scr.shape, x1_scr.dtype)
        h2_scr[...] = jnp.zeros(h2_scr.shape, h2_scr.dtype)
        _stack_small_weights(p["w_sp"], p["w_pool"], wsp_scr, wpool_scr)

    tile_in_seq = lax.rem(jnp.minimum(step, n_tiles - 1), tiles_per_seq)
    stages = _front_units(x_ref, hist_ref, p, front_scr, x1_scr.at[cur], h2_scr.at[cur], hist_out_ref,
                          v_out_ref, S=S, R=R, first_tile=tile_in_seq == 0,
                          pos_base=pos0 + tile_in_seq * R)
    chunks, downs, fins = _ffn_units(x1_scr.at[prev], h2_scr.at[prev], y_ref, p, hid_scr, M=M,
                                     final_norm=final_norm)
    queues = {name: list(units) for name, units in stages.items()}
    queues["ffn"], queues["down"] = list(chunks), list(downs)
    for name in _PIPELINE_ORDER:
        queues[name].pop(0)()
    assert not any(queues.values()), "emission plan does not cover every unit"
    for unit in fins:
        unit()


def _resident(shape, layer, n_grid):
    index = (layer,) + (0,) * (len(shape) - 1)
    index_map = (lambda i: index) if n_grid == 1 else (lambda b, i: index)
    return pl.BlockSpec((None,) + tuple(shape[1:]), index_map, pipeline_mode=pl.Buffered(1))


def _run_layer(x, hist, params, layer, *, S, R, tiles_per_seq, pos0, final_norm, emit_v, pipelined, name):
    n_rows = x.shape[0]
    M = S * R
    n_seq = n_rows // (R * tiles_per_seq)
    n_tiles = n_rows // M
    assert n_seq % S == 0 and (S == 1 or tiles_per_seq == 1)
    assert M % UNIT_ROWS == 0 and (R % UNIT_ROWS == 0 or R == POOL_ROW_CHUNK)
    hist_layer = min(layer, hist.shape[0] - 1)
    shared_hist = hist.shape[1] == 1
    if pipelined:
        assert S == 1 and M == PROMPT_TILE
        last = n_tiles - 1
        grid = (n_tiles + 1,)
        x_map = lambda i: (jnp.minimum(i, last), 0)
        y_map = lambda i: (jnp.maximum(i - 1, 0), 0)
        hist_in_map = lambda i: (hist_layer, 0 if shared_hist else jnp.minimum(i, last) // tiles_per_seq,
                                 0, 0)
        hist_out_map = lambda i: (jnp.minimum(i, last) // tiles_per_seq, 0, 0)
        semantics = ("arbitrary",)
    else:
        grid = (n_seq // S, tiles_per_seq)
        assert tiles_per_seq == 1, "the sequential form carries no pooling history between tiles"
        x_map = y_map = lambda b, i: (b, 0)
        hist_in_map = lambda b, i: (hist_layer, 0 if shared_hist else b, 0, 0)
        hist_out_map = lambda b, i: (b, 0, 0)
        semantics = ("arbitrary", "arbitrary")

    in_specs = [pl.BlockSpec((M, D_MODEL), x_map),
                pl.BlockSpec((None, S, HIST_ROWS, D_POOL), hist_in_map)]
    in_specs += [_resident(params[k].shape, min(layer, params[k].shape[0] - 1), len(grid))
                 for k in _PARAM_NAMES]
    out_shape = [jax.ShapeDtypeStruct((n_rows, D_MODEL), jnp.float32),
                 jax.ShapeDtypeStruct((n_seq, HIST_ROWS, D_POOL), jnp.float32)]
    out_specs = [pl.BlockSpec((M, D_MODEL), y_map),
                 pl.BlockSpec((S, HIST_ROWS, D_POOL), hist_out_map)]
    if emit_v:
        assert not pipelined
        out_shape.append(jax.ShapeDtypeStruct((n_rows, D_GMLP), jnp.float32))
        out_specs.append(pl.BlockSpec((M, D_GMLP), x_map))

    n_slots = 2 if pipelined else 1
    scratch_shapes = [
        pltpu.VMEM((n_slots, M, D_MODEL), jnp.float32),
        pltpu.VMEM((n_slots, M, D_MODEL), jnp.bfloat16),
        pltpu.VMEM((M, D_FF), jnp.bfloat16),
        pltpu.VMEM((M, D_MODEL), jnp.bfloat16),
        pltpu.VMEM((M, D_IN_PROJ), jnp.float32),
        pltpu.VMEM((M, D_GMLP), jnp.float32),
        pltpu.VMEM((M, D_POOL), jnp.bfloat16),
        pltpu.VMEM((M, D_MODEL), jnp.bfloat16),
        pltpu.VMEM((S, HIST_ROWS + R, D_POOL), jnp.float32),
        pltpu.VMEM((GMLP_HEADS // 2, MLP_CHUNK, 2 * MLP_CHUNK), jnp.bfloat16),
        pltpu.VMEM((len(POOL_WINDOWS) // 2, 2 * POOL_GROUP_DIM, 2 * POOL_GROUP_DIM),
                   jnp.bfloat16),
    ]
    kern = functools.partial(_layer_kernel, S=S, R=R, tiles_per_seq=tiles_per_seq, n_tiles=n_tiles,
                             pos0=pos0, final_norm=final_norm, emit_v=emit_v, pipelined=pipelined)
    return pl.pallas_call(
        kern,
        grid=grid,
        in_specs=in_specs,
        out_specs=out_specs,
        out_shape=out_shape,
        scratch_shapes=scratch_shapes,
        compiler_params=pltpu.CompilerParams(
            dimension_semantics=semantics,
            vmem_limit_bytes=VMEM_LIMIT_BYTES),
        name=name,
    )(x, hist, *[params[k] for k in _PARAM_NAMES])


def kernel(x_prompt, x_sample, state_pool, g_mix, w_in, g_v, w_spatial, b_spatial, w_pool, pool_scale,
           g_branch, w_out, g_ffn, w_gate, w_up, w_down, g_final):
    depth = w_in.shape[0]
    batch, seq, _ = x_prompt.shape
    dec_batch, dec_seq, _ = x_sample.shape
    bf16 = jnp.bfloat16

    def rows(g):
        return g.reshape(g.shape[0], 1, g.shape[-1])

    params = {
        "g_mix": rows(g_mix), "w_in": w_in.astype(bf16), "g_v": rows(g_v), "w_sp": w_spatial,
        "b_sp": jnp.repeat(jnp.swapaxes(b_spatial, 1, 2), D_GMLP // GMLP_HEADS, axis=2),
        "w_pool": w_pool.astype(bf16), "pscale": rows(pool_scale), "g_br": rows(g_branch),
        "w_out": w_out.astype(bf16), "g_ffn": rows(g_ffn), "w_gate": w_gate.astype(bf16),
        "w_up": w_up.astype(bf16), "w_down": w_down.astype(bf16), "g_fin": g_final.reshape(1, 1, -1),
    }
    xp = x_prompt.reshape(batch * seq, D_MODEL)
    xs = x_sample.reshape(dec_batch * dec_seq, D_MODEL)
    no_hist = jnp.zeros((1, 1, HIST_ROWS, D_POOL), jnp.float32)
    sample_hist = jnp.pad(state_pool, ((0, 0), (0, 0), (HIST_ROWS - POOL_HIST, 0), (0, 0)))

    hist_p, hist_s, v_s = [], [], []
    for l in range(depth):
        last = l == depth - 1
        xp, hp = _run_layer(xp, no_hist, params, l, S=1, R=PROMPT_TILE, tiles_per_seq=seq // PROMPT_TILE,
                            pos0=0, final_norm=last, emit_v=False, pipelined=True,
                            name=f"prompt_layer{l}")
        xs, hs, vs = _run_layer(xs, sample_hist, params, l, S=dec_batch, R=dec_seq, tiles_per_seq=1,
                                pos0=PAST_LEN, final_norm=last, emit_v=True, pipelined=False,
                                name=f"sample_layer{l}")
        hist_p.append(hp[:, HIST_ROWS - POOL_HIST:, :])
        hist_s.append(hs[:, HIST_ROWS - POOL_HIST:, :])
        v_s.append(vs.reshape(dec_batch, dec_seq, D_GMLP))

    y_prompt = xp.reshape(batch, seq, D_MODEL)
    y_sample = xs.reshape(dec_batch, dec_seq, D_MODEL)
    return (y_prompt, y_sample, jnp.stack(hist_p, axis=0), jnp.stack(hist_s, axis=0),
            jnp.stack(v_s, axis=0))
```

```python
import functools

import jax
import jax.numpy as jnp
from jax import lax
from jax.experimental import pallas as pl
from jax.experimental.pallas import tpu as pltpu

D_MODEL = 1024
D_GMLP = 512
D_POOL = 512
GMLP_HEADS = 8
MLP_CHUNK = 128
CHUNK = 64
POOL_WINDOWS = (2, 4, 8, 16)
POOL_GROUP_DIM = 128
POOL_HIST = 15
HIST_ROWS = 16
D_IN_PROJ = 2 * D_GMLP + D_POOL
D_FF = 2816
EPS = 1e-6
PAST_LEN = 4096

LANES = 128
FF_CHUNK = 256
PROJ_CHUNK = 512
ROW_CHUNK = 32
UNIT_ROWS = 128
PROMPT_TILE = 512
VMEM_LIMIT_BYTES = 56 * 1024 * 1024


def _rms(x, g):
    ms = jnp.mean(x * x, axis=-1, keepdims=True)
    return x * lax.rsqrt(ms + EPS) * g


def _gelu(x):
    return 0.5 * x * (1.0 + lax.erf(x * (2.0 ** -0.5)))


def _silu(x):
    return x / (1.0 + jnp.exp(-x))


def _front_units(x_ref, hist_ref, p, scr, x1_ref, h2_ref, hist_out_ref, v_out_ref, *,
                 S, R, first_tile, pos_base):
    h_scr, z_scr, v_scr, d_scr, ym_scr, xc_scr, wsp_scr, wpool_scr = scr
    M = S * R
    BR = min(R, MLP_CHUNK)
    f32, bf16 = jnp.float32, jnp.bfloat16
    stages = {}

    def rows_units(fn):
        def unit(u0):
            def run():
                for r0 in range(u0, u0 + UNIT_ROWS, ROW_CHUNK):
                    fn(slice(r0, r0 + ROW_CHUNK))
            return run
        return [unit(u0) for u0 in range(0, M, UNIT_ROWS)]

    def norm1(rows):
        h_scr[rows, :] = _rms(x_ref[rows, :], p["g_mix"][...]).astype(bf16)
    stages["norm1"] = rows_units(norm1)

    def in_proj(c0):
        def run():
            cols = slice(c0, c0 + PROJ_CHUNK)
            z_scr[:, cols] = jnp.dot(h_scr[...], p["w_in"][:, cols], preferred_element_type=f32)
        return run
    stages["in_proj"] = [in_proj(D_GMLP), in_proj(0), in_proj(2 * D_GMLP)]

    def v_norm(rows):
        v = _rms(_gelu(z_scr[rows, D_GMLP:2 * D_GMLP]), p["g_v"][...])
        if v_out_ref is not None:
            v_out_ref[rows, :] = v
        v_scr[rows, :] = v
    stages["v_norm"] = rows_units(v_norm)

    def pool_rows():
        for s in range(S):
            if first_tile is None:
                head = hist_ref[s]
            else:
                head = jnp.where(first_tile, hist_ref[s], xc_scr[s, R:R + HIST_ROWS, :])
            xc_scr[s, 0:HIST_ROWS, :] = head
            xc_scr[s, HIST_ROWS:HIST_ROWS + R, :] = z_scr[s * R:(s + 1) * R, 2 * D_GMLP:]
            hist_out_ref[s] = xc_scr[s, R:R + HIST_ROWS, :]
    stages["pool_rows"] = [pool_rows]

    def pool_windows(s, q0, nr):
        def run():
            for g, w in enumerate(POOL_WINDOWS):
                cols = slice(g * POOL_GROUP_DIM, (g + 1) * POOL_GROUP_DIM)
                xc = xc_scr[s, q0:q0 + HIST_ROWS + nr, cols]
                acc = xc
                k = 1
                while k < w:
                    acc = acc + pltpu.roll(acc, k, axis=0)
                    k *= 2
                acc, xb = acc[HIST_ROWS:], xc[HIST_ROWS:]
                mean = acc * (1.0 / w)
                if q0 == 0:
                    pos = pos_base + lax.broadcasted_iota(jnp.int32, (HIST_ROWS, POOL_GROUP_DIM), 0)
                    cnt = jnp.minimum(pos + 1, w).astype(f32)
                    mean = jnp.concatenate([acc[:HIST_ROWS] / cnt, mean[HIST_ROWS:]], axis=0)
                d_scr[s * R + q0:s * R + q0 + nr, cols] = (mean - xb).astype(bf16)
        return run
    stages["pool_windows"] = [pool_windows(s, q0, min(R, UNIT_ROWS))
                              for s in range(S) for q0 in range(0, R, UNIT_ROWS)]

    def spatial(b0):
        def run():
            lane = lax.broadcasted_iota(jnp.int32, (BR, LANES), 1)
            first_head = lane < (LANES // 2)
            blocks = [slice(b0 + j * BR, b0 + (j + 1) * BR) for j in range(2)]
            sp = [[], []]
            for q in range(GMLP_HEADS // 2):
                rhs = []
                for rows in blocks:
                    vq = v_scr[rows, q * LANES:(q + 1) * LANES]
                    lo = jnp.where(first_head, vq, 0.0).astype(bf16)
                    hi = jnp.where(first_head, 0.0, vq).astype(bf16)
                    if BR < MLP_CHUNK:
                        pad = jnp.zeros((MLP_CHUNK - BR, LANES), bf16)
                        rhs.append(jnp.concatenate([lo, pad, hi, pad], axis=0))
                    else:
                        rhs.append(jnp.concatenate([lo, hi], axis=0))
                res = jnp.dot(wsp_scr[q, 0:BR, :], jnp.concatenate(rhs, axis=1),
                              preferred_element_type=f32)
                for j in range(2):
                    sp[j].append(res[:, j * LANES:(j + 1) * LANES])
            for j, rows in enumerate(blocks):
                spj = jnp.concatenate(sp[j], axis=1) + p["b_sp"][0:BR, :]
                ya = _gelu(z_scr[rows, 0:D_GMLP]) * spj
                ym_scr[rows, 0:D_GMLP] = _rms(ya, p["g_br"][:, 0:D_GMLP]).astype(bf16)
        return run
    stages["spatial"] = [spatial(b0) for b0 in range(0, M, 2 * BR)]

    def pool_proj():
        for j in range(len(POOL_WINDOWS) // 2):
            cols = slice(2 * j * POOL_GROUP_DIM, (2 * j + 2) * POOL_GROUP_DIM)
            yb = jnp.dot(d_scr[:, cols], wpool_scr[j], preferred_element_type=f32)
            z_scr[:, 2 * D_GMLP + cols.start:2 * D_GMLP + cols.stop] = yb * p["pscale"][:, cols]
    stages["pool_proj"] = [pool_proj]

    def pool_norm(rows):
        ym_scr[rows, D_GMLP:] = _rms(z_scr[rows, 2 * D_GMLP:], p["g_br"][:, D_GMLP:]).astype(bf16)
    stages["pool_norm"] = rows_units(pool_norm)

    def out_proj(c0):
        def run():
            cols = slice(c0, c0 + PROJ_CHUNK)
            x1_ref[:, cols] = x_ref[:, cols] + jnp.dot(ym_scr[...], p["w_out"][:, cols],
                                                       preferred_element_type=f32)
        return run
    stages["out_proj"] = [out_proj(c0) for c0 in range(0, D_MODEL, PROJ_CHUNK)]

    def norm2(rows):
        h2_ref[rows, :] = _rms(x1_ref[rows, :], p["g_ffn"][...]).astype(bf16)
    stages["norm2"] = rows_units(norm2)
    return stages


def _ffn_units(x1_ref, h2_ref, y_ref, p, hid_scr, *, M, final_norm):
    f32, bf16 = jnp.float32, jnp.bfloat16

    def gate_up(c0):
        def run():
            cols = slice(c0, c0 + FF_CHUNK)
            gate = jnp.dot(h2_ref[...], p["w_gate"][:, cols], preferred_element_type=f32)
            up = jnp.dot(h2_ref[...], p["w_up"][:, cols], preferred_element_type=f32)
            hid_scr[:, cols] = (_silu(gate) * up).astype(bf16)
        return run

    def down(c0):
        def run():
            cols = slice(c0, c0 + PROJ_CHUNK)
            y_ref[:, cols] = x1_ref[:, cols] + jnp.dot(hid_scr[...], p["w_down"][:, cols],
                                                       preferred_element_type=f32)
        return run

    def fin(u0):
        def run():
            for r0 in range(u0, u0 + UNIT_ROWS, ROW_CHUNK):
                rows = slice(r0, r0 + ROW_CHUNK)
                y_ref[rows, :] = _rms(y_ref[rows, :], p["g_fin"][...])
        return run

    chunks = [gate_up(c0) for c0 in range(0, D_FF, FF_CHUNK)]
    downs = [down(c0) for c0 in range(0, D_MODEL, PROJ_CHUNK)]
    fins = [fin(u0) for u0 in range(0, M, UNIT_ROWS)] if final_norm else []
    return chunks, downs, fins


_FRONT_ORDER = ("norm1", "in_proj", "v_norm", "pool_rows", "pool_windows", "spatial", "pool_proj",
                "pool_norm", "out_proj", "norm2")

_PIPELINE_ORDER = (
    "norm1", "norm1", "ffn", "norm1", "norm1", "ffn", "in_proj",
    "v_norm", "in_proj", "v_norm", "ffn", "v_norm", "in_proj", "v_norm", "ffn",
    "pool_rows", "spatial", "ffn", "pool_windows", "pool_windows", "ffn", "spatial", "ffn",
    "pool_windows", "pool_windows", "ffn", "pool_proj", "pool_norm", "pool_norm", "pool_norm", "pool_norm",
    "ffn", "ffn", "ffn", "out_proj", "out_proj", "down", "norm2", "norm2", "norm2", "norm2", "down")


def _stack_small_weights(w_sp_ref, w_pool_ref, wsp_scr, wpool_scr):
    ri = lax.broadcasted_iota(jnp.int32, (MLP_CHUNK, MLP_CHUNK), 0)
    ci = lax.broadcasted_iota(jnp.int32, (MLP_CHUNK, MLP_CHUNK), 1)
    causal = (ri // CHUNK) >= (ci // CHUNK)
    for h in range(GMLP_HEADS):
        half = (h % 2) * MLP_CHUNK
        wsp_scr[h // 2, :, half:half + MLP_CHUNK] = jnp.where(causal, w_sp_ref[h], 0.0).astype(jnp.bfloat16)
    G = POOL_GROUP_DIM
    zeros = jnp.zeros((G, G), jnp.bfloat16)
    for g in range(len(POOL_WINDOWS)):
        a = (g % 2) * G
        wpool_scr[g // 2, a:a + G, a:a + G] = w_pool_ref[g]
        wpool_scr[g // 2, a:a + G, G - a:2 * G - a] = zeros


_PARAM_NAMES = ("g_mix", "w_in", "g_v", "w_sp", "b_sp", "w_pool", "pscale", "g_br", "w_out", "g_ffn",
                "w_gate", "w_up", "w_down", "g_fin")


def _layer_kernel(x_ref, hist_ref, *rest, S, R, tiles_per_seq, n_tiles, pos0, final_norm, emit_v,
                  pipelined):
    p = dict(zip(_PARAM_NAMES, rest))
    rest = rest[len(_PARAM_NAMES):]
    y_ref, hist_out_ref = rest[:2]
    v_out_ref = rest[2] if emit_v else None
    x1_scr, h2_scr, hid_scr = rest[3 if emit_v else 2:][:3]
    front_scr = rest[3 if emit_v else 2:][3:]
    M = S * R

    wsp_scr, wpool_scr = front_scr[-2:]

    if not pipelined:
        _stack_small_weights(p["w_sp"], p["w_pool"], wsp_scr, wpool_scr)
        stages = _front_units(x_ref, hist_ref, p, front_scr, x1_scr.at[0], h2_scr.at[0], hist_out_ref,
                              v_out_ref, S=S, R=R, first_tile=None, pos_base=pos0)
        chunks, downs, fins = _ffn_units(x1_scr.at[0], h2_scr.at[0], y_ref, p, hid_scr.at[0], M=M,
                                         final_norm=final_norm)
        for unit in [u for name in _FRONT_ORDER for u in stages[name]] + chunks + downs + fins:
            unit()
        return

    step = pl.program_id(0)
    cur = lax.rem(step, 2)
    prev = 1 - cur

    @pl.when(step == 0)
    def _():
        x1_scr[...] = jnp.zeros(x1_scr.shape, x1_scr.dtype)
        h2_scr[...] = jnp.zeros(h2_scr.shape, h2_scr.dtype)
        _stack_small_weights(p["w_sp"], p["w_pool"], wsp_scr, wpool_scr)

    tile_in_seq = lax.rem(jnp.minimum(step, n_tiles - 1), tiles_per_seq)
    stages = _front_units(x_ref, hist_ref, p, front_scr, x1_scr.at[cur], h2_scr.at[cur], hist_out_ref,
                          v_out_ref, S=S, R=R, first_tile=tile_in_seq == 0,
                          pos_base=pos0 + tile_in_seq * R)
    chunks, downs, fins = _ffn_units(x1_scr.at[prev], h2_scr.at[prev], y_ref, p, hid_scr.at[prev], M=M,
                                     final_norm=final_norm)
    queues = {name: list(units) for name, units in stages.items()}
    queues["ffn"], queues["down"] = list(chunks), list(downs)
    for name in _PIPELINE_ORDER:
        queues[name].pop(0)()
    assert not any(queues.values()), "emission plan does not cover every unit"
    for unit in fins:
        unit()


def _resident(shape, layer, n_grid):
    index = (layer,) + (0,) * (len(shape) - 1)
    index_map = (lambda i: index) if n_grid == 1 else (lambda b, i: index)
    return pl.BlockSpec((None,) + tuple(shape[1:]), index_map, pipeline_mode=pl.Buffered(1))


def _run_layer(x, hist, params, layer, *, S, R, tiles_per_seq, pos0, final_norm, emit_v, pipelined, name):
    n_rows = x.shape[0]
    M = S * R
    n_seq = n_rows // (R * tiles_per_seq)
    n_tiles = n_rows // M
    assert n_seq % S == 0 and (S == 1 or tiles_per_seq == 1)
    assert M % UNIT_ROWS == 0 and (R % UNIT_ROWS == 0 or UNIT_ROWS % R == 0) and R % HIST_ROWS == 0
    hist_layer = min(layer, hist.shape[0] - 1)
    shared_hist = hist.shape[1] == 1
    if pipelined:
        assert S == 1 and M == PROMPT_TILE
        last = n_tiles - 1
        grid = (n_tiles + 1,)
        x_map = lambda i: (jnp.minimum(i, last), 0)
        y_map = lambda i: (jnp.maximum(i - 1, 0), 0)
        hist_in_map = lambda i: (hist_layer, 0 if shared_hist else jnp.minimum(i, last) // tiles_per_seq,
                                 0, 0)
        hist_out_map = lambda i: (jnp.minimum(i, last) // tiles_per_seq, 0, 0)
        semantics = ("arbitrary",)
    else:
        grid = (n_seq // S, tiles_per_seq)
        assert tiles_per_seq == 1, "the sequential form carries no pooling history between tiles"
        x_map = y_map = lambda b, i: (b, 0)
        hist_in_map = lambda b, i: (hist_layer, 0 if shared_hist else b, 0, 0)
        hist_out_map = lambda b, i: (b, 0, 0)
        semantics = ("arbitrary", "arbitrary")

    in_specs = [pl.BlockSpec((M, D_MODEL), x_map),
                pl.BlockSpec((None, S, HIST_ROWS, D_POOL), hist_in_map)]
    in_specs += [_resident(params[k].shape, min(layer, params[k].shape[0] - 1), len(grid))
                 for k in _PARAM_NAMES]
    out_shape = [jax.ShapeDtypeStruct((n_rows, D_MODEL), jnp.float32),
                 jax.ShapeDtypeStruct((n_seq, HIST_ROWS, D_POOL), jnp.float32)]
    out_specs = [pl.BlockSpec((M, D_MODEL), y_map),
                 pl.BlockSpec((S, HIST_ROWS, D_POOL), hist_out_map)]
    if emit_v:
        assert not pipelined
        out_shape.append(jax.ShapeDtypeStruct((n_rows, D_GMLP), jnp.float32))
        out_specs.append(pl.BlockSpec((M, D_GMLP), x_map))

    n_slots = 2 if pipelined else 1
    scratch_shapes = [
        pltpu.VMEM((n_slots, M, D_MODEL), jnp.float32),
        pltpu.VMEM((n_slots, M, D_MODEL), jnp.bfloat16),
        pltpu.VMEM((n_slots, M, D_FF), jnp.bfloat16),
        pltpu.VMEM((M, D_MODEL), jnp.bfloat16),
        pltpu.VMEM((M, D_IN_PROJ), jnp.float32),
        pltpu.VMEM((M, D_GMLP), jnp.float32),
        pltpu.VMEM((M, D_POOL), jnp.bfloat16),
        pltpu.VMEM((M, D_MODEL), jnp.bfloat16),
        pltpu.VMEM((S, HIST_ROWS + R, D_POOL), jnp.float32),
        pltpu.VMEM((GMLP_HEADS // 2, MLP_CHUNK, 2 * MLP_CHUNK), jnp.bfloat16),
        pltpu.VMEM((len(POOL_WINDOWS) // 2, 2 * POOL_GROUP_DIM, 2 * POOL_GROUP_DIM),
                   jnp.bfloat16),
    ]
    kern = functools.partial(_layer_kernel, S=S, R=R, tiles_per_seq=tiles_per_seq, n_tiles=n_tiles,
                             pos0=pos0, final_norm=final_norm, emit_v=emit_v, pipelined=pipelined)
    return pl.pallas_call(
        kern,
        grid=grid,
        in_specs=in_specs,
        out_specs=out_specs,
        out_shape=out_shape,
        scratch_shapes=scratch_shapes,
        compiler_params=pltpu.CompilerParams(
            dimension_semantics=semantics,
            vmem_limit_bytes=VMEM_LIMIT_BYTES),
        name=name,
    )(x, hist, *[params[k] for k in _PARAM_NAMES])


def kernel(x_prompt, x_sample, state_pool, g_mix, w_in, g_v, w_spatial, b_spatial, w_pool, pool_scale,
           g_branch, w_out, g_ffn, w_gate, w_up, w_down, g_final):
    depth = w_in.shape[0]
    batch, seq, _ = x_prompt.shape
    dec_batch, dec_seq, _ = x_sample.shape
    bf16 = jnp.bfloat16

    def rows(g):
        return g.reshape(g.shape[0], 1, g.shape[-1])

    params = {
        "g_mix": rows(g_mix), "w_in": w_in.astype(bf16), "g_v": rows(g_v), "w_sp": w_spatial,
        "b_sp": jnp.repeat(jnp.swapaxes(b_spatial, 1, 2), D_GMLP // GMLP_HEADS, axis=2),
        "w_pool": w_pool.astype(bf16), "pscale": rows(pool_scale), "g_br": rows(g_branch),
        "w_out": w_out.astype(bf16), "g_ffn": rows(g_ffn), "w_gate": w_gate.astype(bf16),
        "w_up": w_up.astype(bf16), "w_down": w_down.astype(bf16), "g_fin": g_final.reshape(1, 1, -1),
    }
    xp = x_prompt.reshape(batch * seq, D_MODEL)
    xs = x_sample.reshape(dec_batch * dec_seq, D_MODEL)
    no_hist = jnp.zeros((1, 1, HIST_ROWS, D_POOL), jnp.float32)
    sample_hist = jnp.pad(state_pool, ((0, 0), (0, 0), (HIST_ROWS - POOL_HIST, 0), (0, 0)))

    hist_p, hist_s, v_s = [], [], []
    for l in range(depth):
        last = l == depth - 1
        xp, hp = _run_layer(xp, no_hist, params, l, S=1, R=PROMPT_TILE, tiles_per_seq=seq // PROMPT_TILE,
                            pos0=0, final_norm=last, emit_v=False, pipelined=True,
                            name=f"prompt_layer{l}")
        xs, hs, vs = _run_layer(xs, sample_hist, params, l, S=dec_batch, R=dec_seq, tiles_per_seq=1,
                                pos0=PAST_LEN, final_norm=last, emit_v=True, pipelined=False,
                                name=f"sample_layer{l}")
        hist_p.append(hp[:, HIST_ROWS - POOL_HIST:, :])
        hist_s.append(hs[:, HIST_ROWS - POOL_HIST:, :])
        v_s.append(vs.reshape(dec_batch, dec_seq, D_GMLP))

    y_prompt = xp.reshape(batch, seq, D_MODEL)
    y_sample = xs.reshape(dec_batch, dec_seq, D_MODEL)
    return (y_prompt, y_sample, jnp.stack(hist_p, axis=0), jnp.stack(hist_s, axis=0),
            jnp.stack(v_s, axis=0))
```

```python
import functools

import jax
import jax.numpy as jnp
from jax import lax
from jax.experimental import pallas as pl
from jax.experimental.pallas import tpu as pltpu

D_MODEL = 1024
D_GMLP = 512
D_POOL = 512
GMLP_HEADS = 8
MLP_CHUNK = 128
CHUNK = 64
POOL_WINDOWS = (2, 4, 8, 16)
POOL_GROUP_DIM = 128
POOL_HIST = 15
HIST_ROWS = 16
D_IN_PROJ = 2 * D_GMLP + D_POOL
D_FF = 2816
EPS = 1e-6
PAST_LEN = 4096

LANES = 128
FF_CHUNK = 256
PROJ_CHUNK = 512
ROW_CHUNK = 32
UNIT_ROWS = 128
PROMPT_TILE = 512
VMEM_LIMIT_BYTES = 56 * 1024 * 1024


def _rms(x, g):
    ms = jnp.mean(x * x, axis=-1, keepdims=True)
    return x * lax.rsqrt(ms + EPS) * g


def _gelu(x):
    return 0.5 * x * (1.0 + lax.erf(x * (2.0 ** -0.5)))


def _silu(x):
    return x / (1.0 + jnp.exp(-x))


def _front_units(x_ref, hist_ref, p, scr, x1_ref, h2_ref, hist_out_ref, v_out_ref, *,
                 S, R, first_tile, pos_base):
    h_scr, z_scr, v_scr, d_scr, ym_scr, xc_scr, wsp_scr, wpool_scr = scr
    M = S * R
    BR = min(R, MLP_CHUNK)
    f32, bf16 = jnp.float32, jnp.bfloat16
    stages = {}

    def rows_units(fn):
        def unit(u0):
            def run():
                for r0 in range(u0, u0 + UNIT_ROWS, ROW_CHUNK):
                    fn(slice(r0, r0 + ROW_CHUNK))
            return run
        return [unit(u0) for u0 in range(0, M, UNIT_ROWS)]

    def norm1(rows):
        h_scr[rows, :] = _rms(x_ref[rows, :], p["g_mix"][...]).astype(bf16)
    stages["norm1"] = rows_units(norm1)

    def in_proj(c0):
        def run():
            cols = slice(c0, c0 + PROJ_CHUNK)
            z_scr[:, cols] = jnp.dot(h_scr[...], p["w_in"][:, cols], preferred_element_type=f32)
        return run

    def in_proj_pool():
        for s in range(S):
            if first_tile is None:
                head = hist_ref[s]
            else:
                head = jnp.where(first_tile, hist_ref[s], xc_scr[s, R:R + HIST_ROWS, :])
            xc_scr[s, 0:HIST_ROWS, :] = head
        xb = jnp.dot(h_scr[...], p["w_in"][:, 2 * D_GMLP:], preferred_element_type=f32)
        for s in range(S):
            xc_scr[s, HIST_ROWS:HIST_ROWS + R, :] = xb[s * R:(s + 1) * R]
            hist_out_ref[s] = xc_scr[s, R:R + HIST_ROWS, :]
    stages["in_proj"] = [in_proj(D_GMLP), in_proj(0), in_proj_pool]

    def v_norm(rows):
        v = _rms(_gelu(z_scr[rows, D_GMLP:2 * D_GMLP]), p["g_v"][...])
        if v_out_ref is not None:
            v_out_ref[rows, :] = v
        v_scr[rows, :] = v.astype(bf16)
    stages["v_norm"] = rows_units(v_norm)

    def pool_windows(s, q0, nr):
        def run():
            for g, w in enumerate(POOL_WINDOWS):
                cols = slice(g * POOL_GROUP_DIM, (g + 1) * POOL_GROUP_DIM)
                xc = xc_scr[s, q0:q0 + HIST_ROWS + nr, cols]
                acc = xc
                k = 1
                while k < w:
                    acc = acc + pltpu.roll(acc, k, axis=0)
                    k *= 2
                acc, xb = acc[HIST_ROWS:], xc[HIST_ROWS:]
                mean = acc * (1.0 / w)
                if q0 == 0:
                    pos = pos_base + lax.broadcasted_iota(jnp.int32, (HIST_ROWS, POOL_GROUP_DIM), 0)
                    cnt = jnp.minimum(pos + 1, w).astype(f32)
                    mean = jnp.concatenate([acc[:HIST_ROWS] / cnt, mean[HIST_ROWS:]], axis=0)
                d_scr[s * R + q0:s * R + q0 + nr, cols] = (mean - xb).astype(bf16)
        return run
    stages["pool_windows"] = [pool_windows(s, q0, min(R, UNIT_ROWS))
                              for s in range(S) for q0 in range(0, R, UNIT_ROWS)]

    def spatial(b0):
        def run():
            lane = lax.broadcasted_iota(jnp.int32, (BR, LANES), 1)
            first_head = lane < (LANES // 2)
            blocks = [slice(b0 + j * BR, b0 + (j + 1) * BR) for j in range(2)]
            sp = [[], []]
            for q in range(GMLP_HEADS // 2):
                rhs = []
                for rows in blocks:
                    vq = v_scr[rows, q * LANES:(q + 1) * LANES]
                    lo = jnp.where(first_head, vq, jnp.zeros_like(vq))
                    hi = jnp.where(first_head, jnp.zeros_like(vq), vq)
                    if BR < MLP_CHUNK:
                        pad = jnp.zeros((MLP_CHUNK - BR, LANES), bf16)
                        rhs.append(jnp.concatenate([lo, pad, hi, pad], axis=0))
                    else:
                        rhs.append(jnp.concatenate([lo, hi], axis=0))
                res = jnp.dot(wsp_scr[q, 0:BR, :], jnp.concatenate(rhs, axis=1),
                              preferred_element_type=f32)
                for j in range(2):
                    sp[j].append(res[:, j * LANES:(j + 1) * LANES])
            for j, rows in enumerate(blocks):
                spj = jnp.concatenate(sp[j], axis=1) + p["b_sp"][0:BR, :]
                ya = _gelu(z_scr[rows, 0:D_GMLP]) * spj
                ym_scr[rows, 0:D_GMLP] = _rms(ya, p["g_br"][:, 0:D_GMLP]).astype(bf16)
        return run
    stages["spatial"] = [spatial(b0) for b0 in range(0, M, 2 * BR)]

    def pool_proj():
        for j in range(len(POOL_WINDOWS) // 2):
            cols = slice(2 * j * POOL_GROUP_DIM, (2 * j + 2) * POOL_GROUP_DIM)
            yb = jnp.dot(d_scr[:, cols], wpool_scr[j], preferred_element_type=f32)
            z_scr[:, 2 * D_GMLP + cols.start:2 * D_GMLP + cols.stop] = yb * p["pscale"][:, cols]
    stages["pool_proj"] = [pool_proj]

    def pool_norm(rows):
        ym_scr[rows, D_GMLP:] = _rms(z_scr[rows, 2 * D_GMLP:], p["g_br"][:, D_GMLP:]).astype(bf16)
    stages["pool_norm"] = rows_units(pool_norm)

    def out_proj(c0):
        def run():
            cols = slice(c0, c0 + PROJ_CHUNK)
            x1_ref[:, cols] = x_ref[:, cols] + jnp.dot(ym_scr[...], p["w_out"][:, cols],
                                                       preferred_element_type=f32)
        return run
    stages["out_proj"] = [out_proj(c0) for c0 in range(0, D_MODEL, PROJ_CHUNK)]

    def norm2(rows):
        h2_ref[rows, :] = _rms(x1_ref[rows, :], p["g_ffn"][...]).astype(bf16)
    stages["norm2"] = rows_units(norm2)
    return stages


def _ffn_units(x1_ref, h2_ref, y_ref, p, hid_scr, *, M, final_norm):
    f32, bf16 = jnp.float32, jnp.bfloat16

    def gate_up(c0):
        def run():
            cols = slice(c0, c0 + FF_CHUNK)
            gate = jnp.dot(h2_ref[...], p["w_gate"][:, cols], preferred_element_type=f32)
            up = jnp.dot(h2_ref[...], p["w_up"][:, cols], preferred_element_type=f32)
            hid_scr[:, cols] = (_silu(gate) * up).astype(bf16)
        return run

    def down(c0):
        def run():
            cols = slice(c0, c0 + PROJ_CHUNK)
            y_ref[:, cols] = x1_ref[:, cols] + jnp.dot(hid_scr[...], p["w_down"][:, cols],
                                                       preferred_element_type=f32)
        return run

    def fin(u0):
        def run():
            for r0 in range(u0, u0 + UNIT_ROWS, ROW_CHUNK):
                rows = slice(r0, r0 + ROW_CHUNK)
                y_ref[rows, :] = _rms(y_ref[rows, :], p["g_fin"][...])
        return run

    chunks = [gate_up(c0) for c0 in range(0, D_FF, FF_CHUNK)]
    downs = [down(c0) for c0 in range(0, D_MODEL, PROJ_CHUNK)]
    fins = [fin(u0) for u0 in range(0, M, UNIT_ROWS)] if final_norm else []
    return chunks, downs, fins


_FRONT_ORDER = ("norm1", "in_proj", "v_norm", "pool_windows", "spatial", "pool_proj",
                "pool_norm", "out_proj", "norm2")

_PIPELINE_ORDER = (
    "norm1", "norm1", "ffn", "norm1", "norm1", "ffn", "in_proj",
    "v_norm", "in_proj", "v_norm", "ffn", "v_norm", "in_proj", "v_norm", "ffn",
    "spatial", "ffn", "pool_windows", "pool_windows", "ffn", "spatial", "ffn",
    "pool_windows", "pool_windows", "ffn", "pool_proj", "pool_norm", "pool_norm", "pool_norm", "pool_norm",
    "ffn", "ffn", "ffn", "out_proj", "out_proj", "down", "norm2", "norm2", "norm2", "norm2", "down")


def _stack_small_weights(w_sp_ref, w_pool_ref, wsp_scr, wpool_scr):
    ri = lax.broadcasted_iota(jnp.int32, (MLP_CHUNK, MLP_CHUNK), 0)
    ci = lax.broadcasted_iota(jnp.int32, (MLP_CHUNK, MLP_CHUNK), 1)
    causal = (ri // CHUNK) >= (ci // CHUNK)
    for h in range(GMLP_HEADS):
        half = (h % 2) * MLP_CHUNK
        wsp_scr[h // 2, :, half:half + MLP_CHUNK] = jnp.where(causal, w_sp_ref[h], 0.0).astype(jnp.bfloat16)
    G = POOL_GROUP_DIM
    zeros = jnp.zeros((G, G), jnp.bfloat16)
    for g in range(len(POOL_WINDOWS)):
        a = (g % 2) * G
        wpool_scr[g // 2, a:a + G, a:a + G] = w_pool_ref[g]
        wpool_scr[g // 2, a:a + G, G - a:2 * G - a] = zeros


_PARAM_NAMES = ("g_mix", "w_in", "g_v", "w_sp", "b_sp", "w_pool", "pscale", "g_br", "w_out", "g_ffn",
                "w_gate", "w_up", "w_down", "g_fin")


def _layer_kernel(x_ref, hist_ref, *rest, S, R, tiles_per_seq, n_tiles, pos0, final_norm, emit_v,
                  pipelined):
    p = dict(zip(_PARAM_NAMES, rest))
    rest = rest[len(_PARAM_NAMES):]
    y_ref, hist_out_ref = rest[:2]
    v_out_ref = rest[2] if emit_v else None
    x1_scr, h2_scr, hid_scr = rest[3 if emit_v else 2:][:3]
    front_scr = rest[3 if emit_v else 2:][3:]
    M = S * R

    wsp_scr, wpool_scr = front_scr[-2:]

    if not pipelined:
        _stack_small_weights(p["w_sp"], p["w_pool"], wsp_scr, wpool_scr)
        stages = _front_units(x_ref, hist_ref, p, front_scr, x1_scr.at[0], h2_scr.at[0], hist_out_ref,
                              v_out_ref, S=S, R=R, first_tile=None, pos_base=pos0)
        chunks, downs, fins = _ffn_units(x1_scr.at[0], h2_scr.at[0], y_ref, p, hid_scr.at[0], M=M,
                                         final_norm=final_norm)
        for unit in [u for name in _FRONT_ORDER for u in stages[name]] + chunks + downs + fins:
            unit()
        return

    step = pl.program_id(0)
    cur = lax.rem(step, 2)
    prev = 1 - cur

    @pl.when(step == 0)
    def _():
        x1_scr[...] = jnp.zeros(x1_scr.shape, x1_scr.dtype)
        h2_scr[...] = jnp.zeros(h2_scr.shape, h2_scr.dtype)
        _stack_small_weights(p["w_sp"], p["w_pool"], wsp_scr, wpool_scr)

    tile_in_seq = lax.rem(jnp.minimum(step, n_tiles - 1), tiles_per_seq)
    stages = _front_units(x_ref, hist_ref, p, front_scr, x1_scr.at[cur], h2_scr.at[cur], hist_out_ref,
                          v_out_ref, S=S, R=R, first_tile=tile_in_seq == 0,
                          pos_base=pos0 + tile_in_seq * R)
    chunks, downs, fins = _ffn_units(x1_scr.at[prev], h2_scr.at[prev], y_ref, p, hid_scr.at[prev], M=M,
                                     final_norm=final_norm)
    queues = {name: list(units) for name, units in stages.items()}
    queues["ffn"], queues["down"] = list(chunks), list(downs)
    for name in _PIPELINE_ORDER:
        queues[name].pop(0)()
    assert not any(queues.values()), "emission plan does not cover every unit"
    for unit in fins:
        unit()


def _resident(shape, layer, n_grid):
    index = (layer,) + (0,) * (len(shape) - 1)
    index_map = (lambda i: index) if n_grid == 1 else (lambda b, i: index)
    return pl.BlockSpec((None,) + tuple(shape[1:]), index_map, pipeline_mode=pl.Buffered(1))


def _run_layer(x, hist, params, layer, *, S, R, tiles_per_seq, pos0, final_norm, emit_v, pipelined, name):
    n_rows = x.shape[0]
    M = S * R
    n_seq = n_rows // (R * tiles_per_seq)
    n_tiles = n_rows // M
    assert n_seq % S == 0 and (S == 1 or tiles_per_seq == 1)
    assert M % UNIT_ROWS == 0 and (R % UNIT_ROWS == 0 or UNIT_ROWS % R == 0) and R % HIST_ROWS == 0
    hist_layer = min(layer, hist.shape[0] - 1)
    shared_hist = hist.shape[1] == 1
    if pipelined:
        assert S == 1 and M == PROMPT_TILE
        last = n_tiles - 1
        grid = (n_tiles + 1,)
        x_map = lambda i: (jnp.minimum(i, last), 0)
        y_map = lambda i: (jnp.maximum(i - 1, 0), 0)
        hist_in_map = lambda i: (hist_layer, 0 if shared_hist else jnp.minimum(i, last) // tiles_per_seq,
                                 0, 0)
        hist_out_map = lambda i: (jnp.minimum(i, last) // tiles_per_seq, 0, 0)
        semantics = ("arbitrary",)
    else:
        grid = (n_seq // S, tiles_per_seq)
        assert tiles_per_seq == 1, "the sequential form carries no pooling history between tiles"
        x_map = y_map = lambda b, i: (b, 0)
        hist_in_map = lambda b, i: (hist_layer, 0 if shared_hist else b, 0, 0)
        hist_out_map = lambda b, i: (b, 0, 0)
        semantics = ("arbitrary", "arbitrary")

    in_specs = [pl.BlockSpec((M, D_MODEL), x_map),
                pl.BlockSpec((None, S, HIST_ROWS, D_POOL), hist_in_map)]
    in_specs += [_resident(params[k].shape, min(layer, params[k].shape[0] - 1), len(grid))
                 for k in _PARAM_NAMES]
    out_shape = [jax.ShapeDtypeStruct((n_rows, D_MODEL), jnp.float32),
                 jax.ShapeDtypeStruct((n_seq, HIST_ROWS, D_POOL), jnp.float32)]
    out_specs = [pl.BlockSpec((M, D_MODEL), y_map),
                 pl.BlockSpec((S, HIST_ROWS, D_POOL), hist_out_map)]
    if emit_v:
        assert not pipelined
        out_shape.append(jax.ShapeDtypeStruct((n_rows, D_GMLP), jnp.float32))
        out_specs.append(pl.BlockSpec((M, D_GMLP), x_map))

    n_slots = 2 if pipelined else 1
    scratch_shapes = [
        pltpu.VMEM((n_slots, M, D_MODEL), jnp.float32),
        pltpu.VMEM((n_slots, M, D_MODEL), jnp.bfloat16),
        pltpu.VMEM((n_slots, M, D_FF), jnp.bfloat16),
        pltpu.VMEM((M, D_MODEL), jnp.bfloat16),
        pltpu.VMEM((M, D_IN_PROJ), jnp.float32),
        pltpu.VMEM((M, D_GMLP), jnp.bfloat16),
        pltpu.VMEM((M, D_POOL), jnp.bfloat16),
        pltpu.VMEM((M, D_MODEL), jnp.bfloat16),
        pltpu.VMEM((S, HIST_ROWS + R, D_POOL), jnp.float32),
        pltpu.VMEM((GMLP_HEADS // 2, MLP_CHUNK, 2 * MLP_CHUNK), jnp.bfloat16),
        pltpu.VMEM((len(POOL_WINDOWS) // 2, 2 * POOL_GROUP_DIM, 2 * POOL_GROUP_DIM),
                   jnp.bfloat16),
    ]
    kern = functools.partial(_layer_kernel, S=S, R=R, tiles_per_seq=tiles_per_seq, n_tiles=n_tiles,
                             pos0=pos0, final_norm=final_norm, emit_v=emit_v, pipelined=pipelined)
    return pl.pallas_call(
        kern,
        grid=grid,
        in_specs=in_specs,
        out_specs=out_specs,
        out_shape=out_shape,
        scratch_shapes=scratch_shapes,
        compiler_params=pltpu.CompilerParams(
            dimension_semantics=semantics,
            vmem_limit_bytes=VMEM_LIMIT_BYTES),
        name=name,
    )(x, hist, *[params[k] for k in _PARAM_NAMES])


def kernel(x_prompt, x_sample, state_pool, g_mix, w_in, g_v, w_spatial, b_spatial, w_pool, pool_scale,
           g_branch, w_out, g_ffn, w_gate, w_up, w_down, g_final):
    depth = w_in.shape[0]
    batch, seq, _ = x_prompt.shape
    dec_batch, dec_seq, _ = x_sample.shape
    bf16 = jnp.bfloat16

    def rows(g):
        return g.reshape(g.shape[0], 1, g.shape[-1])

    def wide(w):
        return jnp.pad(w, ((0, 0), (0, 0), (0, LANES)))

    params = {
        "g_mix": rows(g_mix), "w_in": w_in.astype(bf16), "g_v": rows(g_v), "w_sp": w_spatial,
        "b_sp": jnp.repeat(jnp.swapaxes(b_spatial, 1, 2), D_GMLP // GMLP_HEADS, axis=2),
        "w_pool": w_pool.astype(bf16), "pscale": rows(pool_scale), "g_br": rows(g_branch),
        "w_out": wide(w_out.astype(bf16)), "g_ffn": rows(g_ffn), "w_gate": w_gate.astype(bf16),
        "w_up": w_up.astype(bf16), "w_down": wide(w_down.astype(bf16)), "g_fin": g_final.reshape(1, 1, -1),
    }
    xp = x_prompt.reshape(batch * seq, D_MODEL)
    xs = x_sample.reshape(dec_batch * dec_seq, D_MODEL)
    no_hist = jnp.zeros((1, 1, HIST_ROWS, D_POOL), jnp.float32)
    sample_hist = jnp.pad(state_pool, ((0, 0), (0, 0), (HIST_ROWS - POOL_HIST, 0), (0, 0)))

    hist_p, hist_s, v_s = [], [], []
    for l in range(depth):
        last = l == depth - 1
        xp, hp = _run_layer(xp, no_hist, params, l, S=1, R=PROMPT_TILE, tiles_per_seq=seq // PROMPT_TILE,
                            pos0=0, final_norm=last, emit_v=False, pipelined=True,
                            name=f"prompt_layer{l}")
        xs, hs, vs = _run_layer(xs, sample_hist, params, l, S=dec_batch, R=dec_seq, tiles_per_seq=1,
                                pos0=PAST_LEN, final_norm=last, emit_v=True, pipelined=False,
                                name=f"sample_layer{l}")
        hist_p.append(hp[:, HIST_ROWS - POOL_HIST:, :])
        hist_s.append(hs[:, HIST_ROWS - POOL_HIST:, :])
        v_s.append(vs.reshape(dec_batch, dec_seq, D_GMLP))

    y_prompt = xp.reshape(batch, seq, D_MODEL)
    y_sample = xs.reshape(dec_batch, dec_seq, D_MODEL)
    return (y_prompt, y_sample, jnp.stack(hist_p, axis=0), jnp.stack(hist_s, axis=0),
            jnp.stack(v_s, axis=0))
```

```python
import functools

import jax
import jax.numpy as jnp
from jax import lax
from jax.experimental import pallas as pl
from jax.experimental.pallas import tpu as pltpu

D_MODEL = 1024
D_GMLP = 512
D_POOL = 512
GMLP_HEADS = 8
MLP_CHUNK = 128
CHUNK = 64
POOL_WINDOWS = (2, 4, 8, 16)
POOL_GROUP_DIM = 128
POOL_HIST = 15
HIST_ROWS = 16
D_IN_PROJ = 2 * D_GMLP + D_POOL
D_FF = 2816
EPS = 1e-6
PAST_LEN = 4096

LANES = 128
FF_CHUNK = 256
PROJ_CHUNK = 512
ROW_CHUNK = 32
UNIT_ROWS = 128
PROMPT_TILE = 512
VMEM_LIMIT_BYTES = 56 * 1024 * 1024


def _rms(x, g):
    ms = jnp.mean(x * x, axis=-1, keepdims=True)
    return x * lax.rsqrt(ms + EPS) * g


def _gelu(x):
    return 0.5 * x * (1.0 + lax.erf(x * (2.0 ** -0.5)))


def _silu(x):
    return x / (1.0 + jnp.exp(-x))


def _front_units(x_ref, hist_ref, p, scr, x1_ref, h2_ref, hist_out_ref, v_out_ref, *,
                 S, R, first_tile, pos_base):
    h_scr, z_scr, v_scr, d_scr, ym_scr, xc_scr, wsp_scr, wpool_scr = scr
    M = S * R
    BR = min(R, MLP_CHUNK)
    f32, bf16 = jnp.float32, jnp.bfloat16
    stages = {}

    def rows_units(fn):
        def unit(u0):
            def run():
                for r0 in range(u0, u0 + UNIT_ROWS, ROW_CHUNK):
                    fn(slice(r0, r0 + ROW_CHUNK))
            return run
        return [unit(u0) for u0 in range(0, M, UNIT_ROWS)]

    def norm1(rows):
        h_scr[rows, :] = _rms(x_ref[rows, :], p["g_mix"][...]).astype(bf16)
    stages["norm1"] = rows_units(norm1)

    def in_proj(c0):
        def run():
            cols = slice(c0, c0 + PROJ_CHUNK)
            z_scr[:, cols] = jnp.dot(h_scr[...], p["w_in"][:, cols], preferred_element_type=f32)
        return run

    def in_proj_pool():
        for s in range(S):
            if first_tile is None:
                head = hist_ref[s]
            else:
                head = jnp.where(first_tile, hist_ref[s], xc_scr[s, R:R + HIST_ROWS, :])
            xc_scr[s, 0:HIST_ROWS, :] = head
        xb = jnp.dot(h_scr[...], p["w_in"][:, 2 * D_GMLP:], preferred_element_type=f32)
        for s in range(S):
            xc_scr[s, HIST_ROWS:HIST_ROWS + R, :] = xb[s * R:(s + 1) * R]
            hist_out_ref[s] = xc_scr[s, R:R + HIST_ROWS, :]
    stages["in_proj"] = [in_proj(D_GMLP), in_proj(0), in_proj_pool]

    def v_norm(rows):
        v = _rms(_gelu(z_scr[rows, D_GMLP:2 * D_GMLP]), p["g_v"][...])
        if v_out_ref is not None:
            v_out_ref[rows, :] = v
        v_scr[rows, :] = v.astype(bf16)
    stages["v_norm"] = rows_units(v_norm)

    def pool_windows(s, q0, nr):
        def run():
            for g, w in enumerate(POOL_WINDOWS):
                cols = slice(g * POOL_GROUP_DIM, (g + 1) * POOL_GROUP_DIM)
                xc = xc_scr[s, q0:q0 + HIST_ROWS + nr, cols]
                acc = xc
                k = 1
                while k < w:
                    acc = acc + pltpu.roll(acc, k, axis=0)
                    k *= 2
                acc, xb = acc[HIST_ROWS:], xc[HIST_ROWS:]
                mean = acc * (1.0 / w)
                if q0 == 0:
                    pos = pos_base + lax.broadcasted_iota(jnp.int32, (HIST_ROWS, POOL_GROUP_DIM), 0)
                    cnt = jnp.minimum(pos + 1, w).astype(f32)
                    mean = jnp.concatenate([acc[:HIST_ROWS] / cnt, mean[HIST_ROWS:]], axis=0)
                d_scr[s * R + q0:s * R + q0 + nr, cols] = (mean - xb).astype(bf16)
        return run
    stages["pool_windows"] = [pool_windows(s, q0, min(R, UNIT_ROWS))
                              for s in range(S) for q0 in range(0, R, UNIT_ROWS)]

    def spatial(b0):
        def run():
            lane = lax.broadcasted_iota(jnp.int32, (BR, LANES), 1)
            first_head = lane < (LANES // 2)
            blocks = [slice(b0 + j * BR, b0 + (j + 1) * BR) for j in range(2)]
            sp = [[], []]
            for q in range(GMLP_HEADS // 2):
                rhs = []
                for rows in blocks:
                    vq = v_scr[rows, q * LANES:(q + 1) * LANES]
                    lo = jnp.where(first_head, vq, jnp.zeros_like(vq))
                    hi = jnp.where(first_head, jnp.zeros_like(vq), vq)
                    if BR < MLP_CHUNK:
                        pad = jnp.zeros((MLP_CHUNK - BR, LANES), bf16)
                        rhs.append(jnp.concatenate([lo, pad, hi, pad], axis=0))
                    else:
                        rhs.append(jnp.concatenate([lo, hi], axis=0))
                res = jnp.dot(wsp_scr[q, 0:BR, :], jnp.concatenate(rhs, axis=1),
                              preferred_element_type=f32)
                for j in range(2):
                    sp[j].append(res[:, j * LANES:(j + 1) * LANES])
            for j, rows in enumerate(blocks):
                spj = jnp.concatenate(sp[j], axis=1) + p["b_sp"][0:BR, :]
                ya = _gelu(z_scr[rows, 0:D_GMLP]) * spj
                ym_scr[rows, 0:D_GMLP] = _rms(ya, p["g_br"][:, 0:D_GMLP]).astype(bf16)
        return run
    stages["spatial"] = [spatial(b0) for b0 in range(0, M, 2 * BR)]

    def pool_proj():
        for j in range(len(POOL_WINDOWS) // 2):
            cols = slice(2 * j * POOL_GROUP_DIM, (2 * j + 2) * POOL_GROUP_DIM)
            yb = jnp.dot(d_scr[:, cols], wpool_scr[j], preferred_element_type=f32)
            z_scr[:, 2 * D_GMLP + cols.start:2 * D_GMLP + cols.stop] = yb * p["pscale"][:, cols]
    stages["pool_proj"] = [pool_proj]

    def pool_norm(rows):
        ym_scr[rows, D_GMLP:] = _rms(z_scr[rows, 2 * D_GMLP:], p["g_br"][:, D_GMLP:]).astype(bf16)
    stages["pool_norm"] = rows_units(pool_norm)

    def out_proj(c0):
        def run():
            cols = slice(c0, c0 + PROJ_CHUNK)
            x1_ref[:, cols] = x_ref[:, cols] + jnp.dot(ym_scr[...], p["w_out"][:, cols],
                                                       preferred_element_type=f32)
        return run
    stages["out_proj"] = [out_proj(c0) for c0 in range(0, D_MODEL, PROJ_CHUNK)]

    def norm2(rows):
        h2_ref[rows, :] = _rms(x1_ref[rows, :], p["g_ffn"][...]).astype(bf16)
    stages["norm2"] = rows_units(norm2)
    return stages


def _ffn_units(x1_ref, h2_ref, y_ref, p, hid_scr, *, M, final_norm):
    f32, bf16 = jnp.float32, jnp.bfloat16

    def gate_up(c0):
        def run():
            cols = slice(c0, c0 + FF_CHUNK)
            gate = jnp.dot(h2_ref[...], p["w_gate"][:, cols], preferred_element_type=f32)
            up = jnp.dot(h2_ref[...], p["w_up"][:, cols], preferred_element_type=f32)
            hid_scr[:, cols] = (_silu(gate) * up).astype(bf16)
        return run

    def down(c0):
        def run():
            cols = slice(c0, c0 + PROJ_CHUNK)
            y_ref[:, cols] = x1_ref[:, cols] + jnp.dot(hid_scr[...], p["w_down"][:, cols],
                                                       preferred_element_type=f32)
        return run

    def down_rows(r0, nr):
        def run():
            rows = slice(r0, r0 + nr)
            y_ref[rows, :] = x1_ref[rows, :] + jnp.dot(hid_scr[rows, :], p["w_down"][:, 0:D_MODEL],
                                                       preferred_element_type=f32)
        return run

    def fin(u0):
        def run():
            for r0 in range(u0, u0 + UNIT_ROWS, ROW_CHUNK):
                rows = slice(r0, r0 + ROW_CHUNK)
                y_ref[rows, :] = _rms(y_ref[rows, :], p["g_fin"][...])
        return run

    chunks = [gate_up(c0) for c0 in range(0, D_FF, FF_CHUNK)]
    if final_norm:
        n_down = D_MODEL // PROJ_CHUNK
        downs = [down_rows(r0, M // n_down) for r0 in range(0, M, M // n_down)]
        fins = [fin(u0) for u0 in range(0, M, UNIT_ROWS)]
    else:
        downs = [down(c0) for c0 in range(0, D_MODEL, PROJ_CHUNK)]
        fins = []
    return chunks, downs, fins


_FRONT_ORDER = ("norm1", "in_proj", "v_norm", "pool_windows", "spatial", "pool_proj",
                "pool_norm", "out_proj", "norm2")

_PIPELINE_ORDER = (
    "norm1", "norm1", "ffn", "norm1", "norm1", "ffn", "in_proj",
    "v_norm", "in_proj", "v_norm", "ffn", "v_norm", "in_proj", "v_norm", "ffn",
    "spatial", "ffn", "pool_windows", "pool_windows", "ffn", "spatial", "ffn",
    "pool_windows", "pool_windows", "ffn", "pool_proj", "pool_norm", "pool_norm", "pool_norm", "pool_norm",
    "ffn", "ffn", "ffn", "out_proj", "out_proj", "down", "norm2", "norm2", "norm2", "norm2", "down")


def _stack_small_weights(w_sp_ref, w_pool_ref, wsp_scr, wpool_scr):
    ri = lax.broadcasted_iota(jnp.int32, (MLP_CHUNK, MLP_CHUNK), 0)
    ci = lax.broadcasted_iota(jnp.int32, (MLP_CHUNK, MLP_CHUNK), 1)
    causal = (ri // CHUNK) >= (ci // CHUNK)
    for h in range(GMLP_HEADS):
        half = (h % 2) * MLP_CHUNK
        wsp_scr[h // 2, :, half:half + MLP_CHUNK] = jnp.where(causal, w_sp_ref[h], 0.0).astype(jnp.bfloat16)
    G = POOL_GROUP_DIM
    zeros = jnp.zeros((G, G), jnp.bfloat16)
    for g in range(len(POOL_WINDOWS)):
        a = (g % 2) * G
        wpool_scr[g // 2, a:a + G, a:a + G] = w_pool_ref[g]
        wpool_scr[g // 2, a:a + G, G - a:2 * G - a] = zeros


_PARAM_NAMES = ("g_mix", "w_in", "g_v", "w_sp", "b_sp", "w_pool", "pscale", "g_br", "w_out", "g_ffn",
                "w_gate", "w_up", "w_down", "g_fin")


def _layer_kernel(x_ref, hist_ref, *rest, S, R, tiles_per_seq, n_tiles, pos0, final_norm, emit_v,
                  pipelined):
    p = dict(zip(_PARAM_NAMES, rest))
    rest = rest[len(_PARAM_NAMES):]
    y_ref, hist_out_ref = rest[:2]
    v_out_ref = rest[2] if emit_v else None
    x1_scr, h2_scr, hid_scr = rest[3 if emit_v else 2:][:3]
    front_scr = rest[3 if emit_v else 2:][3:]
    M = S * R

    wsp_scr, wpool_scr = front_scr[-2:]

    if not pipelined:
        _stack_small_weights(p["w_sp"], p["w_pool"], wsp_scr, wpool_scr)
        stages = _front_units(x_ref, hist_ref, p, front_scr, x1_scr.at[0], h2_scr.at[0], hist_out_ref,
                              v_out_ref, S=S, R=R, first_tile=None, pos_base=pos0)
        chunks, downs, fins = _ffn_units(x1_scr.at[0], h2_scr.at[0], y_ref, p, hid_scr.at[0], M=M,
                                         final_norm=final_norm)
        for unit in [u for name in _FRONT_ORDER for u in stages[name]] + chunks + downs + fins:
            unit()
        return

    step = pl.program_id(0)
    cur = lax.rem(step, 2)
    prev = 1 - cur

    @pl.when(step == 0)
    def _():
        x1_scr[...] = jnp.zeros(x1_scr.shape, x1_scr.dtype)
        h2_scr[...] = jnp.zeros(h2_scr.shape, h2_scr.dtype)
        _stack_small_weights(p["w_sp"], p["w_pool"], wsp_scr, wpool_scr)

    tile_in_seq = lax.rem(jnp.minimum(step, n_tiles - 1), tiles_per_seq)
    stages = _front_units(x_ref, hist_ref, p, front_scr, x1_scr.at[cur], h2_scr.at[cur], hist_out_ref,
                          v_out_ref, S=S, R=R, first_tile=tile_in_seq == 0,
                          pos_base=pos0 + tile_in_seq * R)
    chunks, downs, fins = _ffn_units(x1_scr.at[prev], h2_scr.at[prev], y_ref, p, hid_scr.at[prev], M=M,
                                     final_norm=final_norm)
    queues = {name: list(units) for name, units in stages.items()}
    queues["ffn"], queues["down"] = list(chunks), list(downs)
    for name in _PIPELINE_ORDER:
        queues[name].pop(0)()
    assert not any(queues.values()), "emission plan does not cover every unit"
    for unit in fins:
        unit()


_WIDE_WINDOW = ("w_out", "w_down")


def _resident(shape, layer, n_grid, widen):
    index = (layer,) + (0,) * (len(shape) - 1)
    index_map = (lambda i: index) if n_grid == 1 else (lambda b, i: index)
    block = tuple(shape[1:])
    if widen:
        block = block[:-1] + (block[-1] + LANES,)
    return pl.BlockSpec((None,) + block, index_map, pipeline_mode=pl.Buffered(1))


def _run_layer(x, hist, params, layer, *, S, R, tiles_per_seq, pos0, final_norm, emit_v, pipelined, name):
    n_rows = x.shape[0]
    M = S * R
    n_seq = n_rows // (R * tiles_per_seq)
    n_tiles = n_rows // M
    assert n_seq % S == 0 and (S == 1 or tiles_per_seq == 1)
    assert M % UNIT_ROWS == 0 and (R % UNIT_ROWS == 0 or UNIT_ROWS % R == 0) and R % HIST_ROWS == 0
    hist_layer = min(layer, hist.shape[0] - 1)
    shared_hist = hist.shape[1] == 1
    if pipelined:
        assert S == 1 and M == PROMPT_TILE
        last = n_tiles - 1
        grid = (n_tiles + 1,)
        x_map = lambda i: (jnp.minimum(i, last), 0)
        y_map = lambda i: (jnp.maximum(i - 1, 0), 0)
        hist_in_map = lambda i: (hist_layer, 0 if shared_hist else jnp.minimum(i, last) // tiles_per_seq,
                                 0, 0)
        hist_out_map = lambda i: (jnp.minimum(i, last) // tiles_per_seq, 0, 0)
        semantics = ("arbitrary",)
    else:
        grid = (n_seq // S, tiles_per_seq)
        assert tiles_per_seq == 1, "the sequential form carries no pooling history between tiles"
        x_map = y_map = lambda b, i: (b, 0)
        hist_in_map = lambda b, i: (hist_layer, 0 if shared_hist else b, 0, 0)
        hist_out_map = lambda b, i: (b, 0, 0)
        semantics = ("arbitrary", "arbitrary")

    in_specs = [pl.BlockSpec((M, D_MODEL), x_map),
                pl.BlockSpec((None, S, HIST_ROWS, D_POOL), hist_in_map)]
    in_specs += [_resident(params[k].shape, min(layer, params[k].shape[0] - 1), len(grid),
                           k in _WIDE_WINDOW) for k in _PARAM_NAMES]
    out_shape = [jax.ShapeDtypeStruct((n_rows, D_MODEL), jnp.float32),
                 jax.ShapeDtypeStruct((n_seq, HIST_ROWS, D_POOL), jnp.float32)]
    out_specs = [pl.BlockSpec((M, D_MODEL), y_map),
                 pl.BlockSpec((S, HIST_ROWS, D_POOL), hist_out_map)]
    if emit_v:
        assert not pipelined
        out_shape.append(jax.ShapeDtypeStruct((n_rows, D_GMLP), jnp.float32))
        out_specs.append(pl.BlockSpec((M, D_GMLP), x_map))

    n_slots = 2 if pipelined else 1
    scratch_shapes = [
        pltpu.VMEM((n_slots, M, D_MODEL), jnp.float32),
        pltpu.VMEM((n_slots, M, D_MODEL), jnp.bfloat16),
        pltpu.VMEM((n_slots, M, D_FF), jnp.bfloat16),
        pltpu.VMEM((M, D_MODEL), jnp.bfloat16),
        pltpu.VMEM((M, D_IN_PROJ), jnp.float32),
        pltpu.VMEM((M, D_GMLP), jnp.bfloat16),
        pltpu.VMEM((M, D_POOL), jnp.bfloat16),
        pltpu.VMEM((M, D_MODEL), jnp.bfloat16),
        pltpu.VMEM((S, HIST_ROWS + R, D_POOL), jnp.float32),
        pltpu.VMEM((GMLP_HEADS // 2, MLP_CHUNK, 2 * MLP_CHUNK), jnp.bfloat16),
        pltpu.VMEM((len(POOL_WINDOWS) // 2, 2 * POOL_GROUP_DIM, 2 * POOL_GROUP_DIM),
                   jnp.bfloat16),
    ]
    kern = functools.partial(_layer_kernel, S=S, R=R, tiles_per_seq=tiles_per_seq, n_tiles=n_tiles,
                             pos0=pos0, final_norm=final_norm, emit_v=emit_v, pipelined=pipelined)
    return pl.pallas_call(
        kern,
        grid=grid,
        in_specs=in_specs,
        out_specs=out_specs,
        out_shape=out_shape,
        scratch_shapes=scratch_shapes,
        compiler_params=pltpu.CompilerParams(
            dimension_semantics=semantics,
            vmem_limit_bytes=VMEM_LIMIT_BYTES),
        name=name,
    )(x, hist, *[params[k] for k in _PARAM_NAMES])


def kernel(x_prompt, x_sample, state_pool, g_mix, w_in, g_v, w_spatial, b_spatial, w_pool, pool_scale,
           g_branch, w_out, g_ffn, w_gate, w_up, w_down, g_final):
    depth = w_in.shape[0]
    batch, seq, _ = x_prompt.shape
    dec_batch, dec_seq, _ = x_sample.shape
    bf16 = jnp.bfloat16

    def rows(g):
        return g.reshape(g.shape[0], 1, g.shape[-1])

    params = {
        "g_mix": rows(g_mix), "w_in": w_in.astype(bf16), "g_v": rows(g_v), "w_sp": w_spatial,
        "b_sp": jnp.repeat(jnp.swapaxes(b_spatial, 1, 2), D_GMLP // GMLP_HEADS, axis=2),
        "w_pool": w_pool.astype(bf16), "pscale": rows(pool_scale), "g_br": rows(g_branch),
        "w_out": w_out.astype(bf16), "g_ffn": rows(g_ffn), "w_gate": w_gate.astype(bf16),
        "w_up": w_up.astype(bf16), "w_down": w_down.astype(bf16), "g_fin": g_final.reshape(1, 1, -1),
    }
    xp = x_prompt.reshape(batch * seq, D_MODEL)
    xs = x_sample.reshape(dec_batch * dec_seq, D_MODEL)
    no_hist = jnp.zeros((1, 1, HIST_ROWS, D_POOL), jnp.float32)
    sample_hist = jnp.pad(state_pool, ((0, 0), (0, 0), (HIST_ROWS - POOL_HIST, 0), (0, 0)))

    hist_p, hist_s, v_s = [], [], []
    for l in range(depth):
        last = l == depth - 1
        xp, hp = _run_layer(xp, no_hist, params, l, S=1, R=PROMPT_TILE, tiles_per_seq=seq // PROMPT_TILE,
                            pos0=0, final_norm=last, emit_v=False, pipelined=True,
                            name=f"prompt_layer{l}")
        xs, hs, vs = _run_layer(xs, sample_hist, params, l, S=dec_batch, R=dec_seq, tiles_per_seq=1,
                                pos0=PAST_LEN, final_norm=last, emit_v=True, pipelined=False,
                                name=f"sample_layer{l}")
        hist_p.append(hp[:, HIST_ROWS - POOL_HIST:, :])
        hist_s.append(hs[:, HIST_ROWS - POOL_HIST:, :])
        v_s.append(vs.reshape(dec_batch, dec_seq, D_GMLP))

    y_prompt = xp.reshape(batch, seq, D_MODEL)
    y_sample = xs.reshape(dec_batch, dec_seq, D_MODEL)
    return (y_prompt, y_sample, jnp.stack(hist_p, axis=0), jnp.stack(hist_s, axis=0),
            jnp.stack(v_s, axis=0))
```

```python
import functools

import jax
import jax.numpy as jnp
from jax import lax
from jax.experimental import pallas as pl
from jax.experimental.pallas import tpu as pltpu

D_MODEL = 1024
D_GMLP = 512
D_POOL = 512
GMLP_HEADS = 8
MLP_CHUNK = 128
CHUNK = 64
POOL_WINDOWS = (2, 4, 8, 16)
POOL_GROUP_DIM = 128
POOL_HIST = 15
HIST_ROWS = 16
D_IN_PROJ = 2 * D_GMLP + D_POOL
D_FF = 2816
EPS = 1e-6
PAST_LEN = 4096

LANES = 128
FF_CHUNK = 256
PROJ_CHUNK = 512
ROW_CHUNK = 32
UNIT_ROWS = 128
PROMPT_TILE = 512
VMEM_LIMIT_BYTES = 56 * 1024 * 1024


def _rms(x, g):
    ms = jnp.mean(x * x, axis=-1, keepdims=True)
    return x * lax.rsqrt(ms + EPS) * g


def _gelu(x):
    return 0.5 * x * (1.0 + lax.erf(x * (2.0 ** -0.5)))


def _silu(x):
    return x / (1.0 + jnp.exp(-x))


def _front_units(x_ref, hist_ref, p, scr, x1_ref, h2_ref, hist_out_ref, v_out_ref, *,
                 S, R, first_tile, pos_base):
    h_scr, z_scr, v_scr, d_scr, ym_scr, xc_scr, wsp_scr, wpool_scr = scr
    M = S * R
    BR = min(R, MLP_CHUNK)
    f32, bf16 = jnp.float32, jnp.bfloat16
    stages = {}

    def rows_units(fn):
        def unit(u0):
            def run():
                for r0 in range(u0, u0 + UNIT_ROWS, ROW_CHUNK):
                    fn(slice(r0, r0 + ROW_CHUNK))
            return run
        return [unit(u0) for u0 in range(0, M, UNIT_ROWS)]

    def norm1(rows):
        h_scr[rows, :] = _rms(x_ref[rows, :], p["g_mix"][...]).astype(bf16)
    stages["norm1"] = rows_units(norm1)

    def in_proj(c0):
        def run():
            cols = slice(c0, c0 + PROJ_CHUNK)
            z_scr[:, cols] = jnp.dot(h_scr[...], p["w_in"][:, cols], preferred_element_type=f32)
        return run

    def in_proj_pool():
        for s in range(S):
            if first_tile is None:
                head = hist_ref[s]
            else:
                head = jnp.where(first_tile, hist_ref[s], xc_scr[s, R:R + HIST_ROWS, :])
            xc_scr[s, 0:HIST_ROWS, :] = head
        xb = jnp.dot(h_scr[...], p["w_in"][:, 2 * D_GMLP:], preferred_element_type=f32)
        for s in range(S):
            xc_scr[s, HIST_ROWS:HIST_ROWS + R, :] = xb[s * R:(s + 1) * R]
            hist_out_ref[s] = xc_scr[s, R:R + HIST_ROWS, :]
    stages["in_proj"] = [in_proj(D_GMLP), in_proj(0), in_proj_pool]

    def v_norm(rows):
        v = _rms(_gelu(z_scr[rows, D_GMLP:2 * D_GMLP]), p["g_v"][...])
        if v_out_ref is not None:
            v_out_ref[rows, :] = v
        v_scr[rows, :] = v.astype(bf16)
    stages["v_norm"] = rows_units(v_norm)

    def pool_windows(s, q0, nr):
        def run():
            for g, w in enumerate(POOL_WINDOWS):
                cols = slice(g * POOL_GROUP_DIM, (g + 1) * POOL_GROUP_DIM)
                xc = xc_scr[s, q0:q0 + HIST_ROWS + nr, cols]
                acc = xc
                k = 1
                while k < w:
                    acc = acc + pltpu.roll(acc, k, axis=0)
                    k *= 2
                acc, xb = acc[HIST_ROWS:], xc[HIST_ROWS:]
                mean = acc * (1.0 / w)
                if q0 == 0:
                    pos = pos_base + lax.broadcasted_iota(jnp.int32, (HIST_ROWS, POOL_GROUP_DIM), 0)
                    cnt = jnp.minimum(pos + 1, w).astype(f32)
                    mean = jnp.concatenate([acc[:HIST_ROWS] / cnt, mean[HIST_ROWS:]], axis=0)
                d_scr[s * R + q0:s * R + q0 + nr, cols] = (mean - xb).astype(bf16)
        return run
    stages["pool_windows"] = [pool_windows(s, q0, min(R, UNIT_ROWS))
                              for s in range(S) for q0 in range(0, R, UNIT_ROWS)]

    def spatial(b0):
        def run():
            lane = lax.broadcasted_iota(jnp.int32, (BR, LANES), 1)
            first_head = lane < (LANES // 2)
            blocks = [slice(b0 + j * BR, b0 + (j + 1) * BR) for j in range(2)]
            sp = [[], []]
            for q in range(GMLP_HEADS // 2):
                rhs = []
                for rows in blocks:
                    vq = v_scr[rows, q * LANES:(q + 1) * LANES]
                    lo = jnp.where(first_head, vq, jnp.zeros_like(vq))
                    hi = jnp.where(first_head, jnp.zeros_like(vq), vq)
                    if BR < MLP_CHUNK:
                        pad = jnp.zeros((MLP_CHUNK - BR, LANES), bf16)
                        rhs.append(jnp.concatenate([lo, pad, hi, pad], axis=0))
                    else:
                        rhs.append(jnp.concatenate([lo, hi], axis=0))
                res = jnp.dot(wsp_scr[q, 0:BR, :], jnp.concatenate(rhs, axis=1),
                              preferred_element_type=f32)
                for j in range(2):
                    sp[j].append(res[:, j * LANES:(j + 1) * LANES])
            for j, rows in enumerate(blocks):
                spj = jnp.concatenate(sp[j], axis=1) + p["b_sp"][0:BR, :]
                ya = _gelu(z_scr[rows, 0:D_GMLP]) * spj
                ym_scr[rows, 0:D_GMLP] = _rms(ya, p["g_br"][:, 0:D_GMLP]).astype(bf16)
        return run
    stages["spatial"] = [spatial(b0) for b0 in range(0, M, 2 * BR)]

    def pool_proj():
        for j in range(len(POOL_WINDOWS) // 2):
            cols = slice(2 * j * POOL_GROUP_DIM, (2 * j + 2) * POOL_GROUP_DIM)
            yb = jnp.dot(d_scr[:, cols], wpool_scr[j], preferred_element_type=f32)
            z_scr[:, 2 * D_GMLP + cols.start:2 * D_GMLP + cols.stop] = yb * p["pscale"][:, cols]
    stages["pool_proj"] = [pool_proj]

    def pool_norm(rows):
        ym_scr[rows, D_GMLP:] = _rms(z_scr[rows, 2 * D_GMLP:], p["g_br"][:, D_GMLP:]).astype(bf16)
    stages["pool_norm"] = rows_units(pool_norm)

    def out_proj(c0):
        def run():
            cols = slice(c0, c0 + PROJ_CHUNK)
            x1_ref[:, cols] = x_ref[:, cols] + jnp.dot(ym_scr[...], p["w_out"][:, cols],
                                                       preferred_element_type=f32)
        return run
    stages["out_proj"] = [out_proj(c0) for c0 in range(0, D_MODEL, PROJ_CHUNK)]

    def norm2(rows):
        h2_ref[rows, :] = _rms(x1_ref[rows, :], p["g_ffn"][...]).astype(bf16)
    stages["norm2"] = rows_units(norm2)
    return stages


def _ffn_units(x1_ref, h2_ref, y_ref, p, hid_scr, *, M, final_norm):
    f32, bf16 = jnp.float32, jnp.bfloat16

    def gate_up(c0):
        def run():
            cols = slice(c0, c0 + FF_CHUNK)
            gate = jnp.dot(h2_ref[...], p["w_gate"][:, cols], preferred_element_type=f32)
            up = jnp.dot(h2_ref[...], p["w_up"][:, cols], preferred_element_type=f32)
            hid_scr[:, cols] = (_silu(gate) * up).astype(bf16)
        return run

    def down(c0):
        def run():
            cols = slice(c0, c0 + PROJ_CHUNK)
            y_ref[:, cols] = x1_ref[:, cols] + jnp.dot(hid_scr[...], p["w_down"][:, cols],
                                                       preferred_element_type=f32)
        return run

    def fin(u0):
        def run():
            for r0 in range(u0, u0 + UNIT_ROWS, ROW_CHUNK):
                rows = slice(r0, r0 + ROW_CHUNK)
                y_ref[rows, :] = _rms(y_ref[rows, :], p["g_fin"][...])
        return run

    chunks = [gate_up(c0) for c0 in range(0, D_FF, FF_CHUNK)]
    downs = [down(c0) for c0 in range(0, D_MODEL, PROJ_CHUNK)]
    fins = [fin(u0) for u0 in range(0, M, UNIT_ROWS)] if final_norm else []
    return chunks, downs, fins


_FRONT_ORDER = ("norm1", "in_proj", "v_norm", "pool_windows", "spatial", "pool_proj",
                "pool_norm", "out_proj", "norm2")

_PIPELINE_ORDER = (
    "norm1", "norm1", "ffn", "norm1", "norm1", "ffn", "in_proj",
    "v_norm", "in_proj", "v_norm", "ffn", "v_norm", "in_proj", "v_norm", "ffn",
    "spatial", "ffn", "pool_windows", "pool_windows", "ffn", "spatial", "ffn",
    "pool_windows", "pool_windows", "ffn", "pool_proj", "pool_norm", "pool_norm", "pool_norm", "pool_norm",
    "ffn", "ffn", "ffn", "out_proj", "out_proj", "down", "norm2", "norm2", "norm2", "norm2", "down")


def _stack_small_weights(w_sp_ref, w_pool_ref, wsp_scr, wpool_scr):
    ri = lax.broadcasted_iota(jnp.int32, (MLP_CHUNK, MLP_CHUNK), 0)
    ci = lax.broadcasted_iota(jnp.int32, (MLP_CHUNK, MLP_CHUNK), 1)
    causal = (ri // CHUNK) >= (ci // CHUNK)
    for h in range(GMLP_HEADS):
        half = (h % 2) * MLP_CHUNK
        wsp_scr[h // 2, :, half:half + MLP_CHUNK] = jnp.where(causal, w_sp_ref[h], 0.0).astype(jnp.bfloat16)
    G = POOL_GROUP_DIM
    zeros = jnp.zeros((G, G), jnp.bfloat16)
    for g in range(len(POOL_WINDOWS)):
        a = (g % 2) * G
        wpool_scr[g // 2, a:a + G, a:a + G] = w_pool_ref[g]
        wpool_scr[g // 2, a:a + G, G - a:2 * G - a] = zeros


_PARAM_NAMES = ("g_mix", "w_in", "g_v", "w_sp", "b_sp", "w_pool", "pscale", "g_br", "w_out", "g_ffn",
                "w_gate", "w_up", "w_down", "g_fin")


def _layer_kernel(x_ref, hist_ref, *rest, S, R, tiles_per_seq, n_tiles, pos0, final_norm, emit_v,
                  pipelined):
    p = dict(zip(_PARAM_NAMES, rest))
    rest = rest[len(_PARAM_NAMES):]
    y_ref, hist_out_ref = rest[:2]
    v_out_ref = rest[2] if emit_v else None
    x1_scr, h2_scr, hid_scr = rest[3 if emit_v else 2:][:3]
    front_scr = rest[3 if emit_v else 2:][3:]
    M = S * R

    wsp_scr, wpool_scr = front_scr[-2:]

    if not pipelined:
        _stack_small_weights(p["w_sp"], p["w_pool"], wsp_scr, wpool_scr)
        stages = _front_units(x_ref, hist_ref, p, front_scr, x1_scr.at[0], h2_scr.at[0], hist_out_ref,
                              v_out_ref, S=S, R=R, first_tile=None, pos_base=pos0)
        chunks, downs, fins = _ffn_units(x1_scr.at[0], h2_scr.at[0], y_ref, p, hid_scr.at[0], M=M,
                                         final_norm=final_norm)
        for unit in [u for name in _FRONT_ORDER for u in stages[name]] + chunks + downs + fins:
            unit()
        return

    step = pl.program_id(0)
    cur = lax.rem(step, 2)
    prev = 1 - cur

    @pl.when(step == 0)
    def _():
        x1_scr[...] = jnp.zeros(x1_scr.shape, x1_scr.dtype)
        h2_scr[...] = jnp.zeros(h2_scr.shape, h2_scr.dtype)
        _stack_small_weights(p["w_sp"], p["w_pool"], wsp_scr, wpool_scr)

    tile_in_seq = lax.rem(jnp.minimum(step, n_tiles - 1), tiles_per_seq)
    stages = _front_units(x_ref, hist_ref, p, front_scr, x1_scr.at[cur], h2_scr.at[cur], hist_out_ref,
                          v_out_ref, S=S, R=R, first_tile=tile_in_seq == 0,
                          pos_base=pos0 + tile_in_seq * R)
    chunks, downs, fins = _ffn_units(x1_scr.at[prev], h2_scr.at[prev], y_ref, p, hid_scr.at[prev], M=M,
                                     final_norm=final_norm)
    queues = {name: list(units) for name, units in stages.items()}
    queues["ffn"], queues["down"] = list(chunks), list(downs)
    for name in _PIPELINE_ORDER:
        queues[name].pop(0)()
    assert not any(queues.values()), "emission plan does not cover every unit"
    for unit in fins:
        unit()


_WIDE_WINDOW = ("w_out", "w_down")


def _resident(shape, layer, n_grid, widen):
    index = (layer,) + (0,) * (len(shape) - 1)
    index_map = (lambda i: index) if n_grid == 1 else (lambda b, i: index)
    block = tuple(shape[1:])
    if widen:
        block = block[:-1] + (block[-1] + LANES,)
    return pl.BlockSpec((None,) + block, index_map, pipeline_mode=pl.Buffered(1))


def _run_layer(x, hist, params, layer, *, S, R, tiles_per_seq, pos0, final_norm, emit_v, pipelined, name):
    n_rows = x.shape[0]
    M = S * R
    n_seq = n_rows // (R * tiles_per_seq)
    n_tiles = n_rows // M
    assert n_seq % S == 0 and (S == 1 or tiles_per_seq == 1)
    assert M % UNIT_ROWS == 0 and (R % UNIT_ROWS == 0 or UNIT_ROWS % R == 0) and R % HIST_ROWS == 0
    hist_layer = min(layer, hist.shape[0] - 1)
    shared_hist = hist.shape[1] == 1
    if pipelined:
        assert S == 1 and M == PROMPT_TILE
        last = n_tiles - 1
        grid = (n_tiles + 1,)
        x_map = lambda i: (jnp.minimum(i, last), 0)
        y_map = lambda i: (jnp.maximum(i - 1, 0), 0)
        hist_in_map = lambda i: (hist_layer, 0 if shared_hist else jnp.minimum(i, last) // tiles_per_seq,
                                 0, 0)
        hist_out_map = lambda i: (jnp.minimum(i, last) // tiles_per_seq, 0, 0)
        semantics = ("arbitrary",)
    else:
        grid = (n_seq // S, tiles_per_seq)
        assert tiles_per_seq == 1, "the sequential form carries no pooling history between tiles"
        x_map = y_map = lambda b, i: (b, 0)
        hist_in_map = lambda b, i: (hist_layer, 0 if shared_hist else b, 0, 0)
        hist_out_map = lambda b, i: (b, 0, 0)
        semantics = ("arbitrary", "arbitrary")

    in_specs = [pl.BlockSpec((M, D_MODEL), x_map),
                pl.BlockSpec((None, S, HIST_ROWS, D_POOL), hist_in_map)]
    in_specs += [_resident(params[k].shape, min(layer, params[k].shape[0] - 1), len(grid),
                           k in _WIDE_WINDOW) for k in _PARAM_NAMES]
    out_shape = [jax.ShapeDtypeStruct((n_rows, D_MODEL), jnp.float32),
                 jax.ShapeDtypeStruct((n_seq, HIST_ROWS, D_POOL), jnp.float32)]
    out_specs = [pl.BlockSpec((M, D_MODEL), y_map),
                 pl.BlockSpec((S, HIST_ROWS, D_POOL), hist_out_map)]
    if emit_v:
        assert not pipelined
        out_shape.append(jax.ShapeDtypeStruct((n_rows, D_GMLP), jnp.float32))
        out_specs.append(pl.BlockSpec((M, D_GMLP), x_map))

    n_slots = 2 if pipelined else 1
    scratch_shapes = [
        pltpu.VMEM((n_slots, M, D_MODEL), jnp.float32),
        pltpu.VMEM((n_slots, M, D_MODEL), jnp.bfloat16),
        pltpu.VMEM((n_slots, M, D_FF), jnp.bfloat16),
        pltpu.VMEM((M, D_MODEL), jnp.bfloat16),
        pltpu.VMEM((M, D_IN_PROJ), jnp.float32),
        pltpu.VMEM((M, D_GMLP), jnp.bfloat16),
        pltpu.VMEM((M, D_POOL), jnp.bfloat16),
        pltpu.VMEM((M, D_MODEL), jnp.bfloat16),
        pltpu.VMEM((S, HIST_ROWS + R, D_POOL), jnp.float32),
        pltpu.VMEM((GMLP_HEADS // 2, MLP_CHUNK, 2 * MLP_CHUNK), jnp.bfloat16),
        pltpu.VMEM((len(POOL_WINDOWS) // 2, 2 * POOL_GROUP_DIM, 2 * POOL_GROUP_DIM),
                   jnp.bfloat16),
    ]
    kern = functools.partial(_layer_kernel, S=S, R=R, tiles_per_seq=tiles_per_seq, n_tiles=n_tiles,
                             pos0=pos0, final_norm=final_norm, emit_v=emit_v, pipelined=pipelined)
    return pl.pallas_call(
        kern,
        grid=grid,
        in_specs=in_specs,
        out_specs=out_specs,
        out_shape=out_shape,
        scratch_shapes=scratch_shapes,
        compiler_params=pltpu.CompilerParams(
            dimension_semantics=semantics,
            vmem_limit_bytes=VMEM_LIMIT_BYTES),
        name=name,
    )(x, hist, *[params[k] for k in _PARAM_NAMES])


def kernel(x_prompt, x_sample, state_pool, g_mix, w_in, g_v, w_spatial, b_spatial, w_pool, pool_scale,
           g_branch, w_out, g_ffn, w_gate, w_up, w_down, g_final):
    depth = w_in.shape[0]
    batch, seq, _ = x_prompt.shape
    dec_batch, dec_seq, _ = x_sample.shape
    bf16 = jnp.bfloat16

    def rows(g):
        return g.reshape(g.shape[0], 1, g.shape[-1])

    params = {
        "g_mix": rows(g_mix), "w_in": w_in.astype(bf16), "g_v": rows(g_v), "w_sp": w_spatial,
        "b_sp": jnp.repeat(jnp.swapaxes(b_spatial, 1, 2), D_GMLP // GMLP_HEADS, axis=2),
        "w_pool": w_pool.astype(bf16), "pscale": rows(pool_scale), "g_br": rows(g_branch),
        "w_out": w_out.astype(bf16), "g_ffn": rows(g_ffn), "w_gate": w_gate.astype(bf16),
        "w_up": w_up.astype(bf16), "w_down": w_down.astype(bf16), "g_fin": g_final.reshape(1, 1, -1),
    }
    xp = x_prompt.reshape(batch * seq, D_MODEL)
    xs = x_sample.reshape(dec_batch * dec_seq, D_MODEL)
    no_hist = jnp.zeros((1, 1, HIST_ROWS, D_POOL), jnp.float32)
    sample_hist = jnp.pad(state_pool, ((0, 0), (0, 0), (HIST_ROWS - POOL_HIST, 0), (0, 0)))

    hist_p, hist_s, v_s = [], [], []
    for l in range(depth):
        last = l == depth - 1
        xp, hp = _run_layer(xp, no_hist, params, l, S=1, R=PROMPT_TILE, tiles_per_seq=seq // PROMPT_TILE,
                            pos0=0, final_norm=last, emit_v=False, pipelined=True,
                            name=f"prompt_layer{l}")
        xs, hs, vs = _run_layer(xs, sample_hist, params, l, S=dec_batch, R=dec_seq, tiles_per_seq=1,
                                pos0=PAST_LEN, final_norm=last, emit_v=True, pipelined=False,
                                name=f"sample_layer{l}")
        hist_p.append(hp[:, HIST_ROWS - POOL_HIST:, :])
        hist_s.append(hs[:, HIST_ROWS - POOL_HIST:, :])
        v_s.append(vs.reshape(dec_batch, dec_seq, D_GMLP))

    y_prompt = xp.reshape(batch, seq, D_MODEL)
    y_sample = xs.reshape(dec_batch, dec_seq, D_MODEL)
    return (y_prompt, y_sample, jnp.stack(hist_p, axis=0), jnp.stack(hist_s, axis=0),
            jnp.stack(v_s, axis=0))
```

```python
import functools

import jax
import jax.numpy as jnp
from jax import lax
from jax.experimental import pallas as pl
from jax.experimental.pallas import tpu as pltpu

D_MODEL = 1024
D_GMLP = 512
D_POOL = 512
GMLP_HEADS = 8
MLP_CHUNK = 128
CHUNK = 64
POOL_WINDOWS = (2, 4, 8, 16)
POOL_GROUP_DIM = 128
POOL_HIST = 15
HIST_ROWS = 16
D_IN_PROJ = 2 * D_GMLP + D_POOL
D_FF = 2816
EPS = 1e-6
PAST_LEN = 4096

LANES = 128
FF_CHUNK = 256
PROJ_CHUNK = 512
ROW_CHUNK = 32
UNIT_ROWS = 128
PROMPT_TILE = 512
VMEM_LIMIT_BYTES = 56 * 1024 * 1024


def _rms(x, g):
    ms = jnp.mean(x * x, axis=-1, keepdims=True)
    return x * lax.rsqrt(ms + EPS) * g


def _gelu(x):
    return 0.5 * x * (1.0 + lax.erf(x * (2.0 ** -0.5)))


def _silu(x):
    return x / (1.0 + jnp.exp(-x))


def _front_units(x_ref, hist_ref, p, scr, x1_ref, h2_ref, hist_out_ref, v_out_ref, *,
                 S, R, first_tile, pos_base):
    h_scr, z_scr, v_scr, d_scr, ym_scr, xc_scr, wsp_scr, wpool_scr = scr
    M = S * R
    BR = min(R, MLP_CHUNK)
    f32, bf16 = jnp.float32, jnp.bfloat16
    stages = {}

    def rows_units(fn):
        def unit(u0):
            def run():
                for r0 in range(u0, u0 + UNIT_ROWS, ROW_CHUNK):
                    fn(slice(r0, r0 + ROW_CHUNK))
            return run
        return [unit(u0) for u0 in range(0, M, UNIT_ROWS)]

    def norm1(rows):
        h_scr[rows, :] = _rms(x_ref[rows, :], p["g_mix"][...]).astype(bf16)
    stages["norm1"] = rows_units(norm1)

    def in_proj(c0):
        def run():
            cols = slice(c0, c0 + PROJ_CHUNK)
            z_scr[:, cols] = jnp.dot(h_scr[...], p["w_in"][:, cols], preferred_element_type=f32)
        return run

    def in_proj_pool():
        for s in range(S):
            if first_tile is None:
                head = hist_ref[s]
            else:
                head = jnp.where(first_tile, hist_ref[s], xc_scr[s, R:R + HIST_ROWS, :])
            xc_scr[s, 0:HIST_ROWS, :] = head
        xb = jnp.dot(h_scr[...], p["w_in"][:, 2 * D_GMLP:], preferred_element_type=f32)
        for s in range(S):
            xc_scr[s, HIST_ROWS:HIST_ROWS + R, :] = xb[s * R:(s + 1) * R]
            hist_out_ref[s] = xc_scr[s, R:R + HIST_ROWS, :]
    stages["in_proj"] = [in_proj(D_GMLP), in_proj(0), in_proj_pool]

    def v_norm(rows):
        v = _rms(_gelu(z_scr[rows, D_GMLP:2 * D_GMLP]), p["g_v"][...])
        if v_out_ref is not None:
            v_out_ref[rows, :] = v
        v_scr[rows, :] = v.astype(bf16)
    stages["v_norm"] = rows_units(v_norm)

    def pool_windows(s, q0, nr):
        def run():
            for g, w in enumerate(POOL_WINDOWS):
                cols = slice(g * POOL_GROUP_DIM, (g + 1) * POOL_GROUP_DIM)
                xc = xc_scr[s, q0:q0 + HIST_ROWS + nr, cols]
                acc = xc
                k = 1
                while k < w:
                    acc = acc + pltpu.roll(acc, k, axis=0)
                    k *= 2
                acc, xb = acc[HIST_ROWS:], xc[HIST_ROWS:]
                mean = acc * (1.0 / w)
                if q0 == 0:
                    pos = pos_base + lax.broadcasted_iota(jnp.int32, (HIST_ROWS, POOL_GROUP_DIM), 0)
                    cnt = jnp.minimum(pos + 1, w).astype(f32)
                    mean = jnp.concatenate([acc[:HIST_ROWS] / cnt, mean[HIST_ROWS:]], axis=0)
                d_scr[s * R + q0:s * R + q0 + nr, cols] = (mean - xb).astype(bf16)
        return run
    stages["pool_windows"] = [pool_windows(s, q0, min(R, UNIT_ROWS))
                              for s in range(S) for q0 in range(0, R, UNIT_ROWS)]

    def spatial(b0):
        def run():
            lane = lax.broadcasted_iota(jnp.int32, (BR, LANES), 1)
            first_head = lane < (LANES // 2)
            blocks = [slice(b0 + j * BR, b0 + (j + 1) * BR) for j in range(2)]
            sp = [[], []]
            for q in range(GMLP_HEADS // 2):
                rhs = []
                for rows in blocks:
                    vq = v_scr[rows, q * LANES:(q + 1) * LANES]
                    lo = jnp.where(first_head, vq, jnp.zeros_like(vq))
                    hi = jnp.where(first_head, jnp.zeros_like(vq), vq)
                    if BR < MLP_CHUNK:
                        pad = jnp.zeros((MLP_CHUNK - BR, LANES), bf16)
                        rhs.append(jnp.concatenate([lo, pad, hi, pad], axis=0))
                    else:
                        rhs.append(jnp.concatenate([lo, hi], axis=0))
                res = jnp.dot(wsp_scr[q, 0:BR, :], jnp.concatenate(rhs, axis=1),
                              preferred_element_type=f32)
                for j in range(2):
                    sp[j].append(res[:, j * LANES:(j + 1) * LANES])
            for j, rows in enumerate(blocks):
                spj = jnp.concatenate(sp[j], axis=1) + p["b_sp"][0:BR, :]
                ya = _gelu(z_scr[rows, 0:D_GMLP]) * spj
                ym_scr[rows, 0:D_GMLP] = _rms(ya, p["g_br"][:, 0:D_GMLP]).astype(bf16)
        return run
    stages["spatial"] = [spatial(b0) for b0 in range(0, M, 2 * BR)]

    def pool_proj():
        for j in range(len(POOL_WINDOWS) // 2):
            cols = slice(2 * j * POOL_GROUP_DIM, (2 * j + 2) * POOL_GROUP_DIM)
            yb = jnp.dot(d_scr[:, cols], wpool_scr[j], preferred_element_type=f32)
            z_scr[:, 2 * D_GMLP + cols.start:2 * D_GMLP + cols.stop] = yb * p["pscale"][:, cols]
    stages["pool_proj"] = [pool_proj]

    def pool_norm(rows):
        ym_scr[rows, D_GMLP:] = _rms(z_scr[rows, 2 * D_GMLP:], p["g_br"][:, D_GMLP:]).astype(bf16)
    stages["pool_norm"] = rows_units(pool_norm)

    def out_proj(c0):
        def run():
            cols = slice(c0, c0 + PROJ_CHUNK)
            x1_ref[:, cols] = x_ref[:, cols] + jnp.dot(ym_scr[...], p["w_out"][:, cols],
                                                       preferred_element_type=f32)
        return run
    stages["out_proj"] = [out_proj(c0) for c0 in range(0, D_MODEL, PROJ_CHUNK)]

    def norm2(rows):
        h2_ref[rows, :] = _rms(x1_ref[rows, :], p["g_ffn"][...]).astype(bf16)
    stages["norm2"] = rows_units(norm2)
    return stages


def _ffn_units(x1_ref, h2_ref, y_ref, p, hid_scr, *, M, final_norm):
    f32, bf16 = jnp.float32, jnp.bfloat16

    def gate_up(c0):
        def run():
            cols = slice(c0, c0 + FF_CHUNK)
            gate = jnp.dot(h2_ref[...], p["w_gate"][:, cols], preferred_element_type=f32)
            up = jnp.dot(h2_ref[...], p["w_up"][:, cols], preferred_element_type=f32)
            hid_scr[:, cols] = (_silu(gate) * up).astype(bf16)
        return run

    def down(c0):
        def run():
            cols = slice(c0, c0 + PROJ_CHUNK)
            y_ref[:, cols] = x1_ref[:, cols] + jnp.dot(hid_scr[...], p["w_down"][:, cols],
                                                       preferred_element_type=f32)
        return run

    def fin(u0):
        def run():
            for r0 in range(u0, u0 + UNIT_ROWS, ROW_CHUNK):
                rows = slice(r0, r0 + ROW_CHUNK)
                y_ref[rows, :] = _rms(y_ref[rows, :], p["g_fin"][...])
        return run

    chunks = [gate_up(c0) for c0 in range(0, D_FF, FF_CHUNK)]
    downs = [down(c0) for c0 in range(0, D_MODEL, PROJ_CHUNK)]
    fins = [fin(u0) for u0 in range(0, M, UNIT_ROWS)] if final_norm else []
    return chunks, downs, fins


_FRONT_ORDER = ("norm1", "in_proj", "v_norm", "pool_windows", "spatial", "pool_proj",
                "pool_norm", "out_proj", "norm2")

_PIPELINE_ORDER = (
    "norm1", "norm1", "ffn", "norm1", "norm1", "ffn", "in_proj",
    "v_norm", "in_proj", "v_norm", "ffn", "v_norm", "in_proj", "v_norm", "ffn",
    "pool_windows", "pool_windows", "ffn", "spatial", "ffn", "pool_windows", "pool_windows", "ffn",
    "spatial", "ffn", "pool_proj", "pool_norm", "pool_norm", "pool_norm", "pool_norm",
    "ffn", "ffn", "ffn", "out_proj", "out_proj", "down", "norm2", "norm2", "norm2", "norm2", "down")


def _stack_small_weights(w_sp_ref, w_pool_ref, wsp_scr, wpool_scr):
    ri = lax.broadcasted_iota(jnp.int32, (MLP_CHUNK, MLP_CHUNK), 0)
    ci = lax.broadcasted_iota(jnp.int32, (MLP_CHUNK, MLP_CHUNK), 1)
    causal = (ri // CHUNK) >= (ci // CHUNK)
    for h in range(GMLP_HEADS):
        half = (h % 2) * MLP_CHUNK
        wsp_scr[h // 2, :, half:half + MLP_CHUNK] = jnp.where(causal, w_sp_ref[h], 0.0).astype(jnp.bfloat16)
    G = POOL_GROUP_DIM
    zeros = jnp.zeros((G, G), jnp.bfloat16)
    for g in range(len(POOL_WINDOWS)):
        a = (g % 2) * G
        wpool_scr[g // 2, a:a + G, a:a + G] = w_pool_ref[g]
        wpool_scr[g // 2, a:a + G, G - a:2 * G - a] = zeros


_PARAM_NAMES = ("g_mix", "w_in", "g_v", "w_sp", "b_sp", "w_pool", "pscale", "g_br", "w_out", "g_ffn",
                "w_gate", "w_up", "w_down", "g_fin")


def _layer_kernel(x_ref, hist_ref, *rest, S, R, tiles_per_seq, n_tiles, pos0, final_norm, emit_v,
                  pipelined):
    p = dict(zip(_PARAM_NAMES, rest))
    rest = rest[len(_PARAM_NAMES):]
    y_ref, hist_out_ref = rest[:2]
    v_out_ref = rest[2] if emit_v else None
    x1_scr, h2_scr, hid_scr = rest[3 if emit_v else 2:][:3]
    front_scr = rest[3 if emit_v else 2:][3:]
    M = S * R

    wsp_scr, wpool_scr = front_scr[-2:]

    if not pipelined:
        _stack_small_weights(p["w_sp"], p["w_pool"], wsp_scr, wpool_scr)
        stages = _front_units(x_ref, hist_ref, p, front_scr, x1_scr.at[0], h2_scr.at[0], hist_out_ref,
                              v_out_ref, S=S, R=R, first_tile=None, pos_base=pos0)
        chunks, downs, fins = _ffn_units(x1_scr.at[0], h2_scr.at[0], y_ref, p, hid_scr.at[0], M=M,
                                         final_norm=final_norm)
        for unit in [u for name in _FRONT_ORDER for u in stages[name]] + chunks + downs + fins:
            unit()
        return

    step = pl.program_id(0)
    cur = lax.rem(step, 2)
    prev = 1 - cur

    @pl.when(step == 0)
    def _():
        x1_scr[...] = jnp.zeros(x1_scr.shape, x1_scr.dtype)
        h2_scr[...] = jnp.zeros(h2_scr.shape, h2_scr.dtype)
        _stack_small_weights(p["w_sp"], p["w_pool"], wsp_scr, wpool_scr)

    tile_in_seq = lax.rem(jnp.minimum(step, n_tiles - 1), tiles_per_seq)
    stages = _front_units(x_ref, hist_ref, p, front_scr, x1_scr.at[cur], h2_scr.at[cur], hist_out_ref,
                          v_out_ref, S=S, R=R, first_tile=tile_in_seq == 0,
                          pos_base=pos0 + tile_in_seq * R)
    chunks, downs, fins = _ffn_units(x1_scr.at[prev], h2_scr.at[prev], y_ref, p, hid_scr.at[prev], M=M,
                                     final_norm=final_norm)
    queues = {name: list(units) for name, units in stages.items()}
    queues["ffn"], queues["down"] = list(chunks), list(downs)
    for name in _PIPELINE_ORDER:
        queues[name].pop(0)()
    assert not any(queues.values()), "emission plan does not cover every unit"
    for unit in fins:
        unit()


def _resident(shape, layer, n_grid):
    index = (layer,) + (0,) * (len(shape) - 1)
    index_map = (lambda i: index) if n_grid == 1 else (lambda b, i: index)
    return pl.BlockSpec((None,) + tuple(shape[1:]), index_map, pipeline_mode=pl.Buffered(1))


def _run_layer(x, hist, params, layer, *, S, R, tiles_per_seq, pos0, final_norm, emit_v, pipelined, name):
    n_rows = x.shape[0]
    M = S * R
    n_seq = n_rows // (R * tiles_per_seq)
    n_tiles = n_rows // M
    assert n_seq % S == 0 and (S == 1 or tiles_per_seq == 1)
    assert M % UNIT_ROWS == 0 and (R % UNIT_ROWS == 0 or UNIT_ROWS % R == 0) and R % HIST_ROWS == 0
    hist_layer = min(layer, hist.shape[0] - 1)
    shared_hist = hist.shape[1] == 1
    if pipelined:
        assert S == 1 and M == PROMPT_TILE
        last = n_tiles - 1
        grid = (n_tiles + 1,)
        x_map = lambda i: (jnp.minimum(i, last), 0)
        y_map = lambda i: (jnp.maximum(i - 1, 0), 0)
        hist_in_map = lambda i: (hist_layer, 0 if shared_hist else jnp.minimum(i, last) // tiles_per_seq,
                                 0, 0)
        hist_out_map = lambda i: (jnp.minimum(i, last) // tiles_per_seq, 0, 0)
        semantics = ("arbitrary",)
    else:
        grid = (n_seq // S, tiles_per_seq)
        assert tiles_per_seq == 1, "the sequential form carries no pooling history between tiles"
        x_map = y_map = lambda b, i: (b, 0)
        hist_in_map = lambda b, i: (hist_layer, 0 if shared_hist else b, 0, 0)
        hist_out_map = lambda b, i: (b, 0, 0)
        semantics = ("arbitrary", "arbitrary")

    in_specs = [pl.BlockSpec((M, D_MODEL), x_map),
                pl.BlockSpec((None, S, HIST_ROWS, D_POOL), hist_in_map)]
    in_specs += [_resident(params[k].shape, min(layer, params[k].shape[0] - 1), len(grid))
                 for k in _PARAM_NAMES]
    out_shape = [jax.ShapeDtypeStruct((n_rows, D_MODEL), jnp.float32),
                 jax.ShapeDtypeStruct((n_seq, HIST_ROWS, D_POOL), jnp.float32)]
    out_specs = [pl.BlockSpec((M, D_MODEL), y_map),
                 pl.BlockSpec((S, HIST_ROWS, D_POOL), hist_out_map)]
    if emit_v:
        assert not pipelined
        out_shape.append(jax.ShapeDtypeStruct((n_rows, D_GMLP), jnp.float32))
        out_specs.append(pl.BlockSpec((M, D_GMLP), x_map))

    n_slots = 2 if pipelined else 1
    scratch_shapes = [
        pltpu.VMEM((n_slots, M, D_MODEL), jnp.float32),
        pltpu.VMEM((n_slots, M, D_MODEL), jnp.bfloat16),
        pltpu.VMEM((n_slots, M, D_FF), jnp.bfloat16),
        pltpu.VMEM((M, D_MODEL), jnp.bfloat16),
        pltpu.VMEM((M, D_IN_PROJ), jnp.float32),
        pltpu.VMEM((M, D_GMLP), jnp.bfloat16),
        pltpu.VMEM((M, D_POOL), jnp.bfloat16),
        pltpu.VMEM((M, D_MODEL), jnp.bfloat16),
        pltpu.VMEM((S, HIST_ROWS + R, D_POOL), jnp.float32),
        pltpu.VMEM((GMLP_HEADS // 2, MLP_CHUNK, 2 * MLP_CHUNK), jnp.bfloat16),
        pltpu.VMEM((len(POOL_WINDOWS) // 2, 2 * POOL_GROUP_DIM, 2 * POOL_GROUP_DIM),
                   jnp.bfloat16),
    ]
    kern = functools.partial(_layer_kernel, S=S, R=R, tiles_per_seq=tiles_per_seq, n_tiles=n_tiles,
                             pos0=pos0, final_norm=final_norm, emit_v=emit_v, pipelined=pipelined)
    return pl.pallas_call(
        kern,
        grid=grid,
        in_specs=in_specs,
        out_specs=out_specs,
        out_shape=out_shape,
        scratch_shapes=scratch_shapes,
        compiler_params=pltpu.CompilerParams(
            dimension_semantics=semantics,
            vmem_limit_bytes=VMEM_LIMIT_BYTES),
        name=name,
    )(x, hist, *[params[k] for k in _PARAM_NAMES])


def kernel(x_prompt, x_sample, state_pool, g_mix, w_in, g_v, w_spatial, b_spatial, w_pool, pool_scale,
           g_branch, w_out, g_ffn, w_gate, w_up, w_down, g_final):
    depth = w_in.shape[0]
    batch, seq, _ = x_prompt.shape
    dec_batch, dec_seq, _ = x_sample.shape
    bf16 = jnp.bfloat16

    def rows(g):
        return g.reshape(g.shape[0], 1, g.shape[-1])

    def wide(w):
        return jnp.pad(w, ((0, 0), (0, 0), (0, LANES)))

    params = {
        "g_mix": rows(g_mix), "w_in": w_in.astype(bf16), "g_v": rows(g_v), "w_sp": w_spatial,
        "b_sp": jnp.repeat(jnp.swapaxes(b_spatial, 1, 2), D_GMLP // GMLP_HEADS, axis=2),
        "w_pool": w_pool.astype(bf16), "pscale": rows(pool_scale), "g_br": rows(g_branch),
        "w_out": wide(w_out.astype(bf16)), "g_ffn": rows(g_ffn), "w_gate": w_gate.astype(bf16),
        "w_up": w_up.astype(bf16), "w_down": wide(w_down.astype(bf16)), "g_fin": g_final.reshape(1, 1, -1),
    }
    xp = x_prompt.reshape(batch * seq, D_MODEL)
    xs = x_sample.reshape(dec_batch * dec_seq, D_MODEL)
    no_hist = jnp.zeros((1, 1, HIST_ROWS, D_POOL), jnp.float32)
    sample_hist = jnp.pad(state_pool, ((0, 0), (0, 0), (HIST_ROWS - POOL_HIST, 0), (0, 0)))

    hist_p, hist_s, v_s = [], [], []
    for l in range(depth):
        last = l == depth - 1
        xp, hp = _run_layer(xp, no_hist, params, l, S=1, R=PROMPT_TILE, tiles_per_seq=seq // PROMPT_TILE,
                            pos0=0, final_norm=last, emit_v=False, pipelined=True,
                            name=f"prompt_layer{l}")
        xs, hs, vs = _run_layer(xs, sample_hist, params, l, S=dec_batch, R=dec_seq, tiles_per_seq=1,
                                pos0=PAST_LEN, final_norm=last, emit_v=True, pipelined=False,
                                name=f"sample_layer{l}")
        hist_p.append(hp[:, HIST_ROWS - POOL_HIST:, :])
        hist_s.append(hs[:, HIST_ROWS - POOL_HIST:, :])
        v_s.append(vs.reshape(dec_batch, dec_seq, D_GMLP))

    y_prompt = xp.reshape(batch, seq, D_MODEL)
    y_sample = xs.reshape(dec_batch, dec_seq, D_MODEL)
    return (y_prompt, y_sample, jnp.stack(hist_p, axis=0), jnp.stack(hist_s, axis=0),
            jnp.stack(v_s, axis=0))
```
